```python
import jax, jax.numpy as jnp
from jax import lax
import numpy as np

D_MODEL = 2048
BATCH = 16
SEQ = 256
DEPTH = 4
DEC_BATCH = 2
DEC_SEQ = 4096
PAST_LEN = 512

GRID_W = 64
D_CONV = D_MODEL // 2
CONV_K = 31
CONV_PAD = CONV_K // 2
N_HEADS_GLA = 4
D_GLA_K = D_MODEL // 2
D_GLA_V = D_MODEL
HEAD_K = D_GLA_K // N_HEADS_GLA
HEAD_V = D_GLA_V // N_HEADS_GLA
GATE_RANK = 16
GLA_TAU = 16.0
GLA_CHUNK = 64
N_EXPERTS = 16
EC_CAPACITY_FACTOR = 2
D_FF = D_MODEL
N_MOD = 6
POS_THETA = 10000.0
EPS = 1e-6
SPLIT_SIZES = (D_CONV, D_CONV, D_GLA_K, D_GLA_K, D_GLA_V, D_GLA_V, GATE_RANK, GATE_RANK, D_MODEL, D_MODEL)
SPLIT_IDX = tuple(int(s) for s in np.cumsum(SPLIT_SIZES)[:-1])
D_IN = int(sum(SPLIT_SIZES))

kernel_name = "hybrid_conv_gla_ec_diffusion_step"


def rmsnorm(x, g):
    xf = x.astype(jnp.float32)
    y = xf * lax.rsqrt(jnp.mean(xf * xf, axis=-1, keepdims=True) + EPS)
    return (y * g.astype(jnp.float32)).astype(x.dtype)


def grid_pos_embed(n, d, dtype):
    rows = n // GRID_W
    rr, cc = jnp.meshgrid(jnp.arange(rows, dtype=jnp.float32), jnp.arange(GRID_W, dtype=jnp.float32), indexing="ij")
    rr = rr.reshape(-1, 1)
    cc = cc.reshape(-1, 1)
    quarter = d // 4
    freqs = POS_THETA ** (-jnp.arange(quarter, dtype=jnp.float32) / quarter)
    emb = jnp.concatenate([jnp.sin(rr * freqs), jnp.cos(rr * freqs), jnp.sin(cc * freqs), jnp.cos(cc * freqs)], axis=-1)
    return emb.astype(dtype)


def adaln(cond, w_ada, b_ada):
    mod = jnp.einsum("bd,de->be", jax.nn.silu(cond), w_ada) + b_ada
    return [m[:, None, :] for m in jnp.split(mod, N_MOD, axis=-1)]


def gla_chunked(q, k, v, log_a, s0):
    b, h, n, dk = q.shape
    dv = v.shape[-1]
    nc = n // GLA_CHUNK
    rs = lambda t: t.reshape(b, h, nc, GLA_CHUNK, t.shape[-1])
    q, k, v, log_a = rs(q), rs(k), rs(v), rs(log_a)
    cum = jnp.cumsum(log_a, axis=3)
    cum_end = cum[..., -1:, :]
    q_in = q * jnp.exp(cum)
    k_in = k * jnp.exp(-cum)
    k_end = k * jnp.exp(cum_end - cum)
    mask = jnp.tril(jnp.ones((GLA_CHUNK, GLA_CHUNK), dtype=bool))
    scores = jnp.einsum("bhnck,bhnsk->bhncs", q_in, k_in)
    o_intra = jnp.einsum("bhncs,bhnsv->bhncv", jnp.where(mask, scores, 0.0), v)
    mv = lambda t: jnp.moveaxis(t, 2, 0)
    xs = (mv(q_in), mv(k_end), mv(v), mv(jnp.exp(cum_end[..., 0, :])))

    def step(s, inp):
        qc, kc, vc, dc = inp
        o = jnp.einsum("bhck,bhkv->bhcv", qc, s)
        s = dc[..., None] * s + jnp.einsum("bhck,bhcv->bhkv", kc, vc)
        return s, o

    s_fin, o_inter = lax.scan(step, s0, xs)
    o = o_intra + jnp.moveaxis(o_inter, 0, 2)
    return o.reshape(b, h, n, dv), s_fin


def token_mix(h, s0_f, s0_b, w_in, conv_w, conv_b, conv_norm, w_conv_out,
              gla_gate_w2, gla_gate_b, gla_norm, w_gla_out, w_mix_out):
    bsz, n, _ = h.shape
    proj = jnp.einsum("bnd,de->bne", h, w_in)
    u_val, u_gate, q, k, v, g, z_f, z_b, gate_a, gate_b = jnp.split(proj, SPLIT_IDX, axis=-1)

    u = u_val * jax.nn.sigmoid(u_gate)
    u = lax.conv_general_dilated(u, conv_w[:, None, :].astype(u.dtype), (1,), [(CONV_PAD, CONV_PAD)],
                                 dimension_numbers=("NWC", "WIO", "NWC"),
                                 feature_group_count=D_CONV) + conv_b
    u = jax.nn.silu(rmsnorm(u, conv_norm))
    branch_a = jnp.einsum("bnc,cd->bnd", u, w_conv_out)

    def heads(t, hd):
        return t.reshape(bsz, n, N_HEADS_GLA, hd).transpose(0, 2, 1, 3).astype(jnp.float32)

    la_f = jax.nn.log_sigmoid((jnp.einsum("bnr,rk->bnk", z_f, gla_gate_w2[0]) + gla_gate_b[0]).astype(jnp.float32)) / GLA_TAU
    la_b = jax.nn.log_sigmoid((jnp.einsum("bnr,rk->bnk", z_b, gla_gate_w2[1]) + gla_gate_b[1]).astype(jnp.float32)) / GLA_TAU
    qh = heads(q, HEAD_K) * (HEAD_K ** -0.5)
    kh = heads(k, HEAD_K)
    vh = heads(v, HEAD_V)
    flip = lambda t: jnp.flip(t, axis=2)
    o_f, s_f = gla_chunked(qh, kh, vh, heads(la_f, HEAD_K), s0_f.astype(jnp.float32))
    o_b, s_b = gla_chunked(flip(qh), flip(kh), flip(vh), flip(heads(la_b, HEAD_K)), s0_b.astype(jnp.float32))
    o = rmsnorm(o_f + flip(o_b), gla_norm)
    o = o.transpose(0, 2, 1, 3).reshape(bsz, n, D_GLA_V).astype(h.dtype) * jax.nn.silu(g)
    branch_b = jnp.einsum("bnv,vd->bnd", o, w_gla_out)

    merged = jax.nn.sigmoid(gate_a) * branch_a + jax.nn.sigmoid(gate_b) * branch_b
    out = jnp.einsum("bnd,de->bne", merged, w_mix_out)
    state = jnp.stack([s_f, s_b], axis=1).astype(h.dtype)
    return out, state


def ec_moe(h, w_router, w_gate, w_up, w_down):
    bsz, n, d = h.shape
    cap = (EC_CAPACITY_FACTOR * n) // N_EXPERTS
    logits = jnp.einsum("bnd,de->bne", h, w_router).astype(jnp.float32)
    aff = jax.nn.softmax(logits, axis=-1)
    gates, idx = lax.top_k(jnp.swapaxes(aff, 1, 2), cap)
    xs = jax.vmap(lambda hb, ib: hb[ib])(h, idx)
    act = jax.nn.silu(jnp.einsum("becd,edf->becf", xs, w_gate)) * jnp.einsum("becd,edf->becf", xs, w_up)
    y = jnp.einsum("becf,efd->becd", act, w_down) * gates[..., None].astype(h.dtype)
    out = jax.vmap(lambda yb, ib: jnp.zeros((n, d), h.dtype).at[ib.reshape(-1)].add(yb.reshape(-1, d)))(y, idx)
    return out


def trunk_layer(x, mods, s0_f, s0_b, w_in, conv_w, conv_b, conv_norm, w_conv_out,
                gla_gate_w2, gla_gate_b, gla_norm, w_gla_out, w_mix_out,
                norm_mix, norm_ffn, w_router, w_exp_gate, w_exp_up, w_exp_down):
    shift1, scale1, gate1, shift2, scale2, gate2 = mods
    h = rmsnorm(x, norm_mix) * (1.0 + scale1) + shift1
    mix, state = token_mix(h, s0_f, s0_b, w_in, conv_w, conv_b, conv_norm, w_conv_out,
                           gla_gate_w2, gla_gate_b, gla_norm, w_gla_out, w_mix_out)
    x = x + gate1 * mix
    h = rmsnorm(x, norm_ffn) * (1.0 + scale2) + shift2
    x = x + gate2 * ec_moe(h, w_router, w_exp_gate, w_exp_up, w_exp_down)
    return x, state


def setup_inputs(seed: int = 0) -> dict:
    key = jax.random.key(seed)
    ks = jax.random.split(key, 24)
    f32 = jnp.float32
    nrm = lambda k, shape, scale: jax.random.normal(k, shape, f32) * scale
    return {
        "x_prompt": nrm(ks[0], (BATCH, SEQ, D_MODEL), 1.0),
        "x_sample": nrm(ks[1], (DEC_BATCH, DEC_SEQ, D_MODEL), 1.0),
        "state_gla": nrm(ks[2], (DEC_BATCH, DEPTH, 2, N_HEADS_GLA, HEAD_K, HEAD_V), 0.5),
        "c": nrm(ks[3], (DEC_BATCH, D_MODEL), 1.0),
        "c_ctx": nrm(ks[4], (D_MODEL,), 1.0),
        "w_in": nrm(ks[5], (DEPTH, D_MODEL, D_IN), D_MODEL ** -0.5),
        "conv_w": nrm(ks[6], (DEPTH, CONV_K, D_CONV), CONV_K ** -0.5),
        "conv_b": nrm(ks[7], (DEPTH, D_CONV), 0.02),
        "conv_norm": 1.0 + nrm(ks[8], (DEPTH, D_CONV), 0.05),
        "w_conv_out": nrm(ks[9], (DEPTH, D_CONV, D_MODEL), D_CONV ** -0.5),
        "gla_gate_w2": nrm(ks[10], (DEPTH, 2, GATE_RANK, D_GLA_K), GATE_RANK ** -0.5),
        "gla_gate_b": nrm(ks[11], (DEPTH, 2, D_GLA_K), 0.02),
        "gla_norm": 1.0 + nrm(ks[12], (DEPTH, HEAD_V), 0.05),
        "w_gla_out": nrm(ks[13], (DEPTH, D_GLA_V, D_MODEL), D_GLA_V ** -0.5),
        "w_mix_out": nrm(ks[14], (DEPTH, D_MODEL, D_MODEL), D_MODEL ** -0.5),
        "w_ada": nrm(ks[15], (DEPTH, D_MODEL, N_MOD * D_MODEL), 0.5 * D_MODEL ** -0.5),
        "b_ada": nrm(ks[16], (DEPTH, N_MOD * D_MODEL), 0.02),
        "norm_mix": 1.0 + nrm(ks[17], (DEPTH, D_MODEL), 0.05),
        "norm_ffn": 1.0 + nrm(ks[18], (DEPTH, D_MODEL), 0.05),
        "w_router": nrm(ks[19], (DEPTH, D_MODEL, N_EXPERTS), D_MODEL ** -0.5),
        "w_exp_gate": nrm(ks[20], (DEPTH, N_EXPERTS, D_MODEL, D_FF), D_MODEL ** -0.5),
        "w_exp_up": nrm(ks[21], (DEPTH, N_EXPERTS, D_MODEL, D_FF), D_MODEL ** -0.5),
        "w_exp_down": nrm(ks[22], (DEPTH, N_EXPERTS, D_FF, D_MODEL), D_FF ** -0.5),
        "norm_final": 1.0 + nrm(ks[23], (D_MODEL,), 0.05),
    }


def reference(x_prompt, x_sample, state_gla, c, c_ctx, w_in, conv_w, conv_b, conv_norm, w_conv_out,
              gla_gate_w2, gla_gate_b, gla_norm, w_gla_out, w_mix_out, w_ada, b_ada,
              norm_mix, norm_ffn, w_router, w_exp_gate, w_exp_up, w_exp_down, norm_final):
    bsz_p = x_prompt.shape[0]
    n_lat = x_sample.shape[1]
    xp = x_prompt
    xs = x_sample + grid_pos_embed(n_lat, D_MODEL, x_sample.dtype)[None]
    zero_state = jnp.zeros((bsz_p, N_HEADS_GLA, HEAD_K, HEAD_V), jnp.float32)
    layer_states = []
    for l in range(DEPTH):
        weights = (w_in[l], conv_w[l], conv_b[l], conv_norm[l], w_conv_out[l],
                   gla_gate_w2[l], gla_gate_b[l], gla_norm[l], w_gla_out[l], w_mix_out[l],
                   norm_mix[l], norm_ffn[l], w_router[l], w_exp_gate[l], w_exp_up[l], w_exp_down[l])
        mods_ctx = adaln(c_ctx[None, :], w_ada[l], b_ada[l])
        xp, st = trunk_layer(xp, mods_ctx, zero_state, zero_state, *weights)
        layer_states.append(st)
        mods_lat = adaln(c, w_ada[l], b_ada[l])
        xs, _ = trunk_layer(xs, mods_lat, state_gla[:, l, 0], state_gla[:, l, 1], *weights)
    y_prompt = rmsnorm(xp, norm_final)
    y_sample = rmsnorm(xs, norm_final)
    new_state_gla = jnp.stack(layer_states, axis=1)
    return (y_prompt, y_sample, new_state_gla)
```

```python
import functools

import jax
import jax.numpy as jnp
import numpy as np
from jax import lax
from jax.experimental import pallas as pl
from jax.experimental.pallas import tpu as pltpu

F32 = jnp.float32
BF16 = jnp.bfloat16
I32 = jnp.int32
HI = lax.Precision.HIGHEST

EPS = 1e-6
GRID_W = 64
POS_THETA = 10000.0
GLA_CHUNK = 64
GLA_TAU = 16.0
EC_CAPACITY_FACTOR = 2
N_MOD = 6
CONV_HALO = 16
PREFIX_BLOCK = 256
VMEM_LIMIT = 56 * 1024 * 1024

NT_DIMS = (((1,), (1,)), ((), ()))
TN_DIMS = (((0,), (0,)), ((), ()))


def _cp(*sem):
    return pltpu.CompilerParams(dimension_semantics=sem, vmem_limit_bytes=VMEM_LIMIT)


def _sigmoid(x):
    return 1.0 / (1.0 + jnp.exp(-x))


def _silu(x):
    return x * _sigmoid(x)


def _rms(x):
    return x * lax.rsqrt(jnp.mean(x * x, axis=-1, keepdims=True) + EPS)


def _dot(a, b):
    return jnp.dot(a, b, preferred_element_type=F32)


def _embed_kernel(xp_ref, xs_ref, emb_ref, o_ref, *, nb_p):
    i = pl.program_id(0)

    @pl.when(i < nb_p)
    def _():
        o_ref[...] = xp_ref[...]

    @pl.when(i >= nb_p)
    def _():
        o_ref[...] = xs_ref[...] + emb_ref[...]


def _pos_table(n, d):
    rows = n // GRID_W
    rr, cc = jnp.meshgrid(jnp.arange(rows, dtype=F32), jnp.arange(GRID_W, dtype=F32), indexing="ij")
    rr = rr.reshape(-1, 1)
    cc = cc.reshape(-1, 1)
    quarter = d // 4
    freqs = POS_THETA ** (-jnp.arange(quarter, dtype=F32) / quarter)
    return jnp.concatenate([jnp.sin(rr * freqs), jnp.cos(rr * freqs), jnp.sin(cc * freqs), jnp.cos(cc * freqs)], axis=-1)


def _embed(xp2, xs2, emb, S):
    tp, d = xp2.shape
    ts = xs2.shape[0]
    bm = min(512, S)
    nb_p, nb_seq = tp // bm, S // bm
    return pl.pallas_call(
        functools.partial(_embed_kernel, nb_p=nb_p),
        out_shape=jax.ShapeDtypeStruct((tp + ts, d), F32),
        grid=((tp + ts) // bm,),
        in_specs=[
            pl.BlockSpec((bm, d), lambda i: (jnp.minimum(i, nb_p - 1), 0)),
            pl.BlockSpec((bm, d), lambda i: (jnp.maximum(i - nb_p, 0), 0)),
            pl.BlockSpec((bm, d), lambda i: (jnp.maximum(i - nb_p, 0) % nb_seq, 0)),
        ],
        out_specs=pl.BlockSpec((bm, d), lambda i: (i, 0)),
        compiler_params=_cp("arbitrary"),
        name="embed",
    )(xp2, xs2, emb)


def _ada_kernel(c_ref, w_ref, b_ref, o_ref):
    s = _silu(c_ref[...])
    o_ref[0] = jnp.dot(s, w_ref[0], precision=HI, preferred_element_type=F32) + b_ref[0]


def _adaln(cond8, w_ada, b_ada):
    depth, d, n = w_ada.shape
    bn = 1024
    return pl.pallas_call(
        _ada_kernel,
        out_shape=jax.ShapeDtypeStruct((depth, 8, n), F32),
        grid=(depth, n // bn),
        in_specs=[
            pl.BlockSpec((8, d), lambda l, j: (0, 0)),
            pl.BlockSpec((1, d, bn), lambda l, j: (l, 0, j)),
            pl.BlockSpec((1, 1, bn), lambda l, j: (l, 0, j)),
        ],
        out_specs=pl.BlockSpec((1, 8, bn), lambda l, j: (l, 0, j)),
        compiler_params=_cp("arbitrary", "arbitrary"),
        name="adaln",
    )(cond8, w_ada, b_ada.reshape(depth, 1, n))


def _in_proj_kernel(x_ref, nw_ref, sh_ref, sc_ref, w_ref, wz_ref, o_ref, z_ref, h_scr):
    @pl.when(pl.program_id(1) == 0)
    def _():
        h = (_rms(x_ref[...]) * nw_ref[...]) * (1.0 + sc_ref[...]) + sh_ref[...]
        hb = h.astype(BF16)
        h_scr[...] = hb
        z_ref[...] = _dot(hb, wz_ref[...])

    o_ref[...] = _dot(h_scr[...], w_ref[...]).astype(o_ref.dtype)


def _in_proj(x, norm_w, mods, w_main, w_z, S):
    t, d = x.shape
    n = w_main.shape[1]
    bm, bn = min(1024, S), 1024
    seg = lambda i: (i * bm) // S
    return pl.pallas_call(
        _in_proj_kernel,
        out_shape=(jax.ShapeDtypeStruct((t, n), BF16), jax.ShapeDtypeStruct((t, 128), F32)),
        grid=(t // bm, n // bn),
        in_specs=[
            pl.BlockSpec((bm, d), lambda i, j: (i, 0)),
            pl.BlockSpec((1, d), lambda i, j: (0, 0)),
            pl.BlockSpec((None, 1, d), lambda i, j: (seg(i) * N_MOD + 0, 0, 0)),
            pl.BlockSpec((None, 1, d), lambda i, j: (seg(i) * N_MOD + 1, 0, 0)),
            pl.BlockSpec((d, bn), lambda i, j: (0, j)),
            pl.BlockSpec((d, 128), lambda i, j: (0, 0)),
        ],
        out_specs=(pl.BlockSpec((bm, bn), lambda i, j: (i, j)), pl.BlockSpec((bm, 128), lambda i, j: (i, 0))),
        scratch_shapes=[pltpu.VMEM((bm, d), BF16)],
        compiler_params=_cp("arbitrary", "arbitrary"),
        name="in_proj",
    )(x, norm_w, mods, mods, w_main, w_z)


CONV_TR, CONV_TC = 64, 256


def _conv_kernel(uv, ug, uvp, ugp, uvn, ugn, cw_ref, cb_ref, cn_ref, o_ref, buf, cv, *, tb, S, Lp, Ls, K):
    row0 = pl.program_id(0) * tb
    pos = jnp.where(row0 < S, lax.rem(row0, Lp), lax.rem(row0 - S, Ls))
    seq_len = jnp.where(row0 < S, Lp, Ls)
    at_start = pos == 0
    at_end = pos + tb == seq_len
    c = uv.shape[1]
    pad = K // 2

    def glu(a, b):
        return a[...].astype(F32) * _sigmoid(b[...].astype(F32))

    buf[0:CONV_HALO, :] = jnp.where(at_start, 0.0, glu(uvp, ugp))
    buf[CONV_HALO:CONV_HALO + tb, :] = glu(uv, ug)
    buf[CONV_HALO + tb:, :] = jnp.where(at_end, 0.0, glu(uvn, ugn))

    for r0 in range(0, tb, CONV_TR):
        for c0 in range(0, c, CONV_TC):
            acc = jnp.broadcast_to(cb_ref[:, c0:c0 + CONV_TC], (CONV_TR, CONV_TC))
            for k in range(K):
                start = CONV_HALO - pad + k + r0
                acc = acc + cw_ref[k:k + 1, c0:c0 + CONV_TC] * buf[start:start + CONV_TR, c0:c0 + CONV_TC]
            cv[r0:r0 + CONV_TR, c0:c0 + CONV_TC] = acc

    y = _rms(cv[...]) * cn_ref[...]
    o_ref[...] = _silu(y).astype(o_ref.dtype)


def _conv_branch(proj, conv_w, conv_b, conv_norm, S, Lp, Ls):
    t = proj.shape[0]
    k, c = conv_w.shape
    tb = 256
    nh = tb // CONV_HALO
    last = t // CONV_HALO - 1
    main = lambda col: pl.BlockSpec((tb, c), lambda i: (i, col))
    prev = lambda col: pl.BlockSpec((CONV_HALO, c), lambda i: (jnp.maximum(i * nh - 1, 0), col))
    nxt = lambda col: pl.BlockSpec((CONV_HALO, c), lambda i: (jnp.minimum((i + 1) * nh, last), col))
    return pl.pallas_call(
        functools.partial(_conv_kernel, tb=tb, S=S, Lp=Lp, Ls=Ls, K=k),
        out_shape=jax.ShapeDtypeStruct((t, c), BF16),
        grid=(t // tb,),
        in_specs=[main(0), main(1), prev(0), prev(1), nxt(0), nxt(1),
                  pl.BlockSpec((k, c), lambda i: (0, 0)),
                  pl.BlockSpec((1, c), lambda i: (0, 0)),
                  pl.BlockSpec((1, c), lambda i: (0, 0))],
        out_specs=pl.BlockSpec((tb, c), lambda i: (i, 0)),
        scratch_shapes=[pltpu.VMEM((tb + 2 * CONV_HALO, c), F32), pltpu.VMEM((tb, c), F32)],
        compiler_params=_cp("arbitrary"),
        name="conv_branch",
    )(proj, proj, proj, proj, proj, proj, conv_w, conv_b.reshape(1, c), conv_norm.reshape(1, c))


def _log_sigmoid(x):
    return jnp.minimum(x, 0.0) - jnp.log1p(jnp.exp(-jnp.abs(x)))


def _gla_chunk(q, k, v, z, w2, b2, st_ref, tri, end, scale):
    la = _log_sigmoid(jnp.dot(z, w2, precision=HI, preferred_element_type=F32) + b2) / GLA_TAU
    cum = jnp.dot(tri, la, precision=HI, preferred_element_type=F32)
    cum_end = cum[end:end + 1, :]
    k32 = k.astype(F32)
    q_in = ((q.astype(F32) * scale) * jnp.exp(cum)).astype(BF16)
    k_in = (k32 * jnp.exp(-cum)).astype(BF16)
    k_end = (k32 * jnp.exp(cum_end - cum)).astype(BF16)
    scores = lax.dot_general(q_in, k_in, NT_DIMS, preferred_element_type=F32)
    p = jnp.where(tri > 0.0, scores, 0.0).astype(BF16)
    st = st_ref[...]
    o = _dot(p, v) + lax.dot_general(q_in, st.astype(BF16), NT_DIMS, preferred_element_type=F32)
    st_ref[...] = st * jnp.exp(cum_end) + lax.dot_general(v, k_end, TN_DIMS, preferred_element_type=F32)
    return o


def _gla_kernel(*refs, cs, nsteps, scale, has_init, n_aliased, has_final):
    qf, kf, vf, zf, qb, kb, vb, zb, w2_ref, b2_ref, gn_ref = refs[:11]
    rest = list(refs[11:])
    s0_ref = rest.pop(0) if has_init else None
    rest = rest[n_aliased:]
    o_ref = rest.pop(0)
    sfin_ref = rest.pop(0) if has_final else None
    o_acc, stf, stb = rest

    c = pl.program_id(2)
    C = GLA_CHUNK

    @pl.when(c == 0)
    def _():
        o_acc[...] = jnp.zeros_like(o_acc)
        if has_init:
            stf[...] = s0_ref[0].T
            stb[...] = s0_ref[1].T
        else:
            stf[...] = jnp.zeros_like(stf)
            stb[...] = jnp.zeros_like(stb)

    r = lax.broadcasted_iota(I32, (C, C), 0)
    s = lax.broadcasted_iota(I32, (C, C), 1)
    tril = jnp.where(s <= r, 1.0, 0.0).astype(F32)
    triu = jnp.where(s >= r, 1.0, 0.0).astype(F32)
    w2f, w2b = w2_ref[0], w2_ref[1]
    b2f, b2b = b2_ref[0], b2_ref[1]
    nc = cs * nsteps

    for j in range(cs):
        jb = cs - 1 - j
        rf = slice(j * C, (j + 1) * C)
        rb = slice(jb * C, (jb + 1) * C)
        of = _gla_chunk(qf[rf, :], kf[rf, :], vf[rf, :], zf[rf, :], w2f, b2f, stf, tril, C - 1, scale)
        ob = _gla_chunk(qb[rb, :], kb[rb, :], vb[rb, :], zb[rb, :], w2b, b2b, stb, triu, 0, scale)
        gf = c * cs + j
        pf = pl.multiple_of(gf * C, C)
        pb = pl.multiple_of((nc - 1 - gf) * C, C)
        o_acc[pl.ds(pf, C), :] += of
        o_acc[pl.ds(pb, C), :] += ob

    @pl.when(c == nsteps - 1)
    def _():
        o_ref[...] = (_rms(o_acc[...]) * gn_ref[...]).astype(o_ref.dtype)
        if has_final:
            sfin_ref[0] = stf[...].T
            sfin_ref[1] = stb[...].T


def _gla(proj, z, w2pad, b2, gla_norm, *, row_off, n_seq, L, H, HK, HV, col_q, col_k, col_v,
         s0=None, s0_layer=0, o_prev=None, ns_prev=None, ns_shape=None, ns_layer=0):
    t = proj.shape[0]
    cs = 4
    tc = cs * GLA_CHUNK
    nsteps = L // tc
    rb0 = row_off // tc
    has_init = s0 is not None
    has_final = ns_shape is not None
    fwd = lambda b, h, c: rb0 + b * nsteps + c
    bwd = lambda b, h, c: rb0 + b * nsteps + nsteps - 1 - c

    def specs(rowmap):
        return [pl.BlockSpec((tc, HK), lambda b, h, c: (rowmap(b, h, c), col_q + h)),
                pl.BlockSpec((tc, HK), lambda b, h, c: (rowmap(b, h, c), col_k + h)),
                pl.BlockSpec((tc, HV), lambda b, h, c: (rowmap(b, h, c), col_v + h)),
                pl.BlockSpec((tc, 128), lambda b, h, c: (rowmap(b, h, c), 0))]

    in_specs = specs(fwd) + specs(bwd) + [
        pl.BlockSpec((2, 128, HK), lambda b, h, c: (0, 0, h)),
        pl.BlockSpec((2, 1, HK), lambda b, h, c: (0, 0, h)),
        pl.BlockSpec((1, HV), lambda b, h, c: (0, 0)),
    ]
    args = [proj, proj, proj, z, proj, proj, proj, z, w2pad, b2, gla_norm.reshape(1, HV)]
    if has_init:
        in_specs.append(pl.BlockSpec((None, None, 2, None, HK, HV), lambda b, h, c: (b, s0_layer, 0, h, 0, 0)))
        args.append(s0)
    aliases = {}
    out_shape = [jax.ShapeDtypeStruct((t, H * HV), F32)]
    out_specs = [pl.BlockSpec((L, HV), lambda b, h, c: (row_off // L + b, h))]
    if o_prev is not None:
        aliases[len(args)] = 0
        in_specs.append(pl.BlockSpec(memory_space=pl.ANY))
        args.append(o_prev)
    if has_final:
        out_shape.append(jax.ShapeDtypeStruct(ns_shape, F32))
        out_specs.append(pl.BlockSpec((None, None, 2, None, HK, HV), lambda b, h, c: (b, ns_layer, 0, h, 0, 0)))
        if ns_prev is not None:
            aliases[len(args)] = 1
            in_specs.append(pl.BlockSpec(memory_space=pl.ANY))
            args.append(ns_prev)
    out = pl.pallas_call(
        functools.partial(_gla_kernel, cs=cs, nsteps=nsteps, scale=float(HK) ** -0.5,
                          has_init=has_init, n_aliased=len(aliases), has_final=has_final),
        out_shape=tuple(out_shape),
        grid=(n_seq, H, nsteps),
        in_specs=in_specs,
        out_specs=tuple(out_specs),
        scratch_shapes=[pltpu.VMEM((L, HV), F32), pltpu.VMEM((HV, HK), F32), pltpu.VMEM((HV, HK), F32)],
        input_output_aliases=aliases,
        compiler_params=_cp("arbitrary", "arbitrary", "arbitrary"),
        name="gla_init" if has_init else "gla_zero",
    )(*args)
    return out


def _branch_kernel(u_ref, o_ref, g_ref, ga_ref, gb_ref, wc_ref, wg_ref, m_ref, og_scr):
    @pl.when(pl.program_id(1) == 0)
    def _():
        og_scr[...] = (o_ref[...] * _silu(g_ref[...].astype(F32))).astype(BF16)

    a = _dot(u_ref[...], wc_ref[...])
    b = _dot(og_scr[...], wg_ref[...])
    m = _sigmoid(ga_ref[...].astype(F32)) * a + _sigmoid(gb_ref[...].astype(F32)) * b
    m_ref[...] = m.astype(m_ref.dtype)


def _branches(u_act, o_norm, proj, w_conv_out, w_gla_out, col_g, col_ga, col_gb):
    t, c = u_act.shape
    d = w_conv_out.shape[1]
    dv = w_gla_out.shape[0]
    bm, bn = 512, 1024
    nb = d // bn
    return pl.pallas_call(
        _branch_kernel,
        out_shape=jax.ShapeDtypeStruct((t, d), BF16),
        grid=(t // bm, nb),
        in_specs=[
            pl.BlockSpec((bm, c), lambda i, j: (i, 0)),
            pl.BlockSpec((bm, dv), lambda i, j: (i, 0)),
            pl.BlockSpec((bm, dv), lambda i, j: (i, col_g)),
            pl.BlockSpec((bm, bn), lambda i, j: (i, col_ga * nb + j)),
            pl.BlockSpec((bm, bn), lambda i, j: (i, col_gb * nb + j)),
            pl.BlockSpec((c, bn), lambda i, j: (0, j)),
            pl.BlockSpec((dv, bn), lambda i, j: (0, j)),
        ],
        out_specs=pl.BlockSpec((bm, bn), lambda i, j: (i, j)),
        scratch_shapes=[pltpu.VMEM((bm, dv), BF16)],
        compiler_params=_cp("arbitrary", "arbitrary"),
        name="branches",
    )(u_act, o_norm, proj, proj, proj, w_conv_out, w_gla_out)


def _mix_out_kernel(m_ref, w_ref, x_ref, g1_ref, nw_ref, sh_ref, sc_ref, wr_ref, xo_ref, h_ref, lg_ref):
    x = x_ref[...] + g1_ref[...] * _dot(m_ref[...], w_ref[...])
    xo_ref[...] = x
    h = (_rms(x) * nw_ref[...]) * (1.0 + sc_ref[...]) + sh_ref[...]
    h_ref[...] = h.astype(BF16)
    lg_ref[...] = lax.dot_general(wr_ref[...], h, NT_DIMS, precision=HI, preferred_element_type=F32)


def _mix_out(merged, w_mix, x, mods, norm_w, w_router_t, S):
    t, d = x.shape
    e = w_router_t.shape[0]
    bm = 512
    seg = lambda i: (i * bm) // S
    mod = lambda which: pl.BlockSpec((None, 1, d), lambda i: (seg(i) * N_MOD + which, 0, 0))
    return pl.pallas_call(
        _mix_out_kernel,
        out_shape=(jax.ShapeDtypeStruct((t, d), F32), jax.ShapeDtypeStruct((t, d), BF16),
                   jax.ShapeDtypeStruct((e, t), F32)),
        grid=(t // bm,),
        in_specs=[
            pl.BlockSpec((bm, d), lambda i: (i, 0)),
            pl.BlockSpec((d, d), lambda i: (0, 0)),
            pl.BlockSpec((bm, d), lambda i: (i, 0)),
            mod(2),
            pl.BlockSpec((1, d), lambda i: (0, 0)),
            mod(3), mod(4),
            pl.BlockSpec((e, d), lambda i: (0, 0)),
        ],
        out_specs=(pl.BlockSpec((bm, d), lambda i: (i, 0)), pl.BlockSpec((bm, d), lambda i: (i, 0)),
                   pl.BlockSpec((e, bm), lambda i: (0, i))),
        compiler_params=_cp("arbitrary"),
        name="mix_out",
    )(merged, w_mix, x, mods, norm_w, mods, mods, w_router_t)


def _route_kernel(lg_ref, slot_ref, gate_ref, *, n_sets, L, cap):
    lg = lg_ref[...]
    e = lg.shape[0]
    ex = jnp.exp(lg - jnp.max(lg, axis=0, keepdims=True))
    aff = ex / jnp.sum(ex, axis=0, keepdims=True)
    bits = pltpu.bitcast(aff, I32)
    sets = [bits[:, s * L:(s + 1) * L] for s in range(n_sets)]

    def body(it, vs):
        bit = jnp.left_shift(jnp.int32(1), 30 - it)
        out = []
        for a, v in zip(sets, vs):
            cand = v | bit
            cnt = jnp.sum(jnp.where(a >= cand, 1.0, 0.0), axis=1, keepdims=True)
            out.append(jnp.where(cnt >= cap, cand, v))
        return tuple(out)

    thr = lax.fori_loop(0, 31, body, tuple(jnp.zeros((e, 1), I32) for _ in range(n_sets)))

    pb = PREFIX_BLOCK
    r = lax.broadcasted_iota(I32, (pb, pb), 0)
    c = lax.broadcasted_iota(I32, (pb, pb), 1)
    tri = jnp.where(r <= c, 1.0, 0.0).astype(BF16)
    for s in range(n_sets):
        a, v = sets[s], thr[s]
        n_gt = jnp.sum(jnp.where(a > v, 1.0, 0.0), axis=1, keepdims=True)
        need = cap - n_gt
        carry_eq = jnp.zeros((e, 1), F32)
        carry_sel = jnp.zeros((e, 1), F32)
        for b0 in range(0, L, pb):
            ab = a[:, b0:b0 + pb]
            eq = jnp.where(ab == v, 1.0, 0.0)
            rank_eq = _dot(eq.astype(BF16), tri) - eq + carry_eq
            sel = jnp.where(ab > v, 1.0, jnp.where(rank_eq < need, eq, 0.0))
            rank_sel = _dot(sel.astype(BF16), tri) - sel + carry_sel
            carry_eq = carry_eq + jnp.sum(eq, axis=1, keepdims=True)
            carry_sel = carry_sel + jnp.sum(sel, axis=1, keepdims=True)
            cols = slice(s * L + b0, s * L + b0 + pb)
            slot_ref[:, cols] = jnp.where(sel > 0.0, rank_sel.astype(I32) + s * cap, -1)
            gate_ref[:, cols] = jnp.where(sel > 0.0, aff[:, cols], 0.0)


def _route(logits_t, *, col0, n_groups, n_sets, L, cap):
    e = logits_t.shape[0]
    n = n_sets * L
    return pl.pallas_call(
        functools.partial(_route_kernel, n_sets=n_sets, L=L, cap=cap),
        out_shape=(jax.ShapeDtypeStruct((e, n_groups * n), I32), jax.ShapeDtypeStruct((e, n_groups * n), F32)),
        grid=(n_groups,),
        in_specs=[pl.BlockSpec((e, n), lambda g: (0, col0 + g))],
        out_specs=(pl.BlockSpec((e, n), lambda g: (0, g)), pl.BlockSpec((e, n), lambda g: (0, g))),
        compiler_params=_cp("arbitrary"),
        name="route",
    )(logits_t)


GATHER_CHUNK = 1024


def _gather_kernel(h_ref, slot_ref, gate_ref, g_ref, gs_ref, acc, *, nslot):
    s_tok = h_ref.shape[0]
    tk = min(GATHER_CHUNK, s_tok)
    gs = jnp.zeros((nslot, 1), F32)
    for t0 in range(0, s_tok, tk):
        slot = slot_ref[:, t0:t0 + tk]
        hit = slot == lax.broadcasted_iota(I32, (nslot, tk), 0)
        onehot = jnp.where(hit, 1.0, 0.0).astype(BF16)
        part = _dot(onehot, h_ref[t0:t0 + tk, :])
        if t0 == 0:
            acc[...] = part
        else:
            acc[...] += part
        gs = gs + jnp.sum(jnp.where(hit, gate_ref[:, t0:t0 + tk], 0.0), axis=1, keepdims=True)
    g_ref[...] = acc[...].astype(g_ref.dtype)
    gs_ref[...] = gs


def _gather(h2, slot3, gate3, S, nslot):
    t, d = h2.shape
    e = slot3.shape[0]
    nseg = t // S
    return pl.pallas_call(
        functools.partial(_gather_kernel, nslot=nslot),
        out_shape=(jax.ShapeDtypeStruct((e, nseg * nslot, d), BF16), jax.ShapeDtypeStruct((e, nseg * nslot, 1), F32)),
        grid=(nseg, e),
        in_specs=[
            pl.BlockSpec((S, d), lambda s, x: (s, 0)),
            pl.BlockSpec((None, 1, S), lambda s, x: (x, 0, s)),
            pl.BlockSpec((None, 1, S), lambda s, x: (x, 0, s)),
        ],
        out_specs=(pl.BlockSpec((None, nslot, d), lambda s, x: (x, s, 0)),
                   pl.BlockSpec((None, nslot, 1), lambda s, x: (x, s, 0))),
        scratch_shapes=[pltpu.VMEM((nslot, d), F32)],
        compiler_params=_cp("arbitrary", "arbitrary"),
        name="moe_gather",
    )(h2, slot3, gate3)


def _ffn_up_kernel(g_ref, wg_ref, wu_ref, a_ref):
    g = g_ref[...]
    a = _dot(g, wg_ref[...].astype(BF16))
    b = _dot(g, wu_ref[...].astype(BF16))
    a_ref[...] = (_silu(a) * b).astype(a_ref.dtype)


def _ffn_up(gathered, w_gate, w_up, layer):
    e, m, d = gathered.shape
    f = w_gate.shape[3]
    bf = 512
    return pl.pallas_call(
        _ffn_up_kernel,
        out_shape=jax.ShapeDtypeStruct((e, m, f), BF16),
        grid=(e, f // bf),
        in_specs=[
            pl.BlockSpec((None, m, d), lambda x, j: (x, 0, 0)),
            pl.BlockSpec((None, None, d, bf), lambda x, j: (layer, x, 0, j)),
            pl.BlockSpec((None, None, d, bf), lambda x, j: (layer, x, 0, j)),
        ],
        out_specs=pl.BlockSpec((None, m, bf), lambda x, j: (x, 0, j)),
        compiler_params=_cp("arbitrary", "arbitrary"),
        name="ffn_up",
    )(gathered, w_gate, w_up)


def _ffn_down_kernel(a_ref, wd_ref, gs_ref, y_ref):
    y = _dot(a_ref[...], wd_ref[...].astype(BF16))
    y_ref[...] = (y * gs_ref[...]).astype(y_ref.dtype)


def _ffn_down(act, w_down, gslot, layer):
    e, m, f = act.shape
    d = w_down.shape[3]
    bn = 512
    return pl.pallas_call(
        _ffn_down_kernel,
        out_shape=jax.ShapeDtypeStruct((e, m, d), BF16),
        grid=(e, d // bn),
        in_specs=[
            pl.BlockSpec((None, m, f), lambda x, j: (x, 0, 0)),
            pl.BlockSpec((None, None, f, bn), lambda x, j: (layer, x, 0, j)),
            pl.BlockSpec((None, m, 1), lambda x, j: (x, 0, 0)),
        ],
        out_specs=pl.BlockSpec((None, m, bn), lambda x, j: (x, 0, j)),
        compiler_params=_cp("arbitrary", "arbitrary"),
        name="ffn_down",
    )(act, w_down, gslot)


def _scatter_kernel(slot_ref, y_ref, x_ref, g2_ref, o_ref, *, nslot):
    x = pl.program_id(1)
    tb = o_ref.shape[0]
    hit = slot_ref[...] == lax.broadcasted_iota(I32, (nslot, tb), 0)
    onehot = jnp.where(hit, 1.0, 0.0).astype(BF16)
    part = lax.dot_general(onehot, y_ref[...], TN_DIMS, preferred_element_type=F32)

    @pl.when(x == 0)
    def _():
        o_ref[...] = part

    @pl.when(x > 0)
    def _():
        o_ref[...] += part

    @pl.when(x == pl.num_programs(1) - 1)
    def _():
        o_ref[...] = x_ref[...] + g2_ref[...] * o_ref[...]


def _scatter(slot3, yg, x, mods, S, nslot):
    t, d = x.shape
    e = slot3.shape[0]
    tb = 512
    seg = lambda i: (i * tb) // S
    return pl.pallas_call(
        functools.partial(_scatter_kernel, nslot=nslot),
        out_shape=jax.ShapeDtypeStruct((t, d), F32),
        grid=(t // tb, e),
        in_specs=[
            pl.BlockSpec((None, 1, tb), lambda i, x: (x, 0, i)),
            pl.BlockSpec((None, nslot, d), lambda i, x: (x, seg(i), 0)),
            pl.BlockSpec((tb, d), lambda i, x: (i, 0)),
            pl.BlockSpec((None, 1, d), lambda i, x: (seg(i) * N_MOD + 5, 0, 0)),
        ],
        out_specs=pl.BlockSpec((tb, d), lambda i, x: (i, 0)),
        compiler_params=_cp("arbitrary", "arbitrary"),
        name="moe_scatter",
    )(slot3, yg, x, mods)


def _final_norm_kernel(x_ref, w_ref, o_ref):
    o_ref[...] = _rms(x_ref[...]) * w_ref[...]


def _final_norm(x, w, row_off, rows):
    d = x.shape[1]
    bm = 512
    off = row_off // bm
    return pl.pallas_call(
        _final_norm_kernel,
        out_shape=jax.ShapeDtypeStruct((rows, d), F32),
        grid=(rows // bm,),
        in_specs=[pl.BlockSpec((bm, d), lambda i: (off + i, 0)), pl.BlockSpec((1, d), lambda i: (0, 0))],
        out_specs=pl.BlockSpec((bm, d), lambda i: (i, 0)),
        compiler_params=_cp("arbitrary"),
        name="final_norm",
    )(x, w)


def kernel(x_prompt, x_sample, state_gla, c, c_ctx, w_in, conv_w, conv_b, conv_norm, w_conv_out, gla_gate_w2, gla_gate_b, gla_norm, w_gla_out, w_mix_out, w_ada, b_ada, norm_mix, norm_ffn, w_router, w_exp_gate, w_exp_up, w_exp_down, norm_final):
    B, SEQ, D = x_prompt.shape
    DB, DSEQ, _ = x_sample.shape
    DEPTH = w_in.shape[0]
    H, HK, HV = state_gla.shape[3:]
    DC = conv_w.shape[2]
    DK, DV = H * HK, H * HV
    RANK = gla_gate_w2.shape[2]
    E = w_router.shape[2]
    S = DSEQ
    assert B * SEQ == S, "prompt tokens must fill exactly one segment"
    assert 2 * RANK <= 128 and conv_w.shape[1] // 2 <= CONV_HALO
    nseg = 1 + DB
    assert nseg <= 8
    T = nseg * S
    cap_p = (EC_CAPACITY_FACTOR * SEQ) // E
    cap_s = (EC_CAPACITY_FACTOR * DSEQ) // E
    nslot = B * cap_p
    assert nslot == cap_s

    x = _embed(x_prompt.reshape(B * SEQ, D), x_sample.reshape(DB * DSEQ, D), _pos_table(DSEQ, D), S)

    cond8 = jnp.zeros((8, D), F32).at[0].set(c_ctx).at[1:1 + DB].set(c)
    mods_all = _adaln(cond8, w_ada, b_ada).reshape(DEPTH, 8 * N_MOD, 1, D)

    z0 = 2 * DC + 2 * DK + 2 * DV
    col_q, col_k = (2 * DC) // HK, (2 * DC + DK) // HK
    col_v = (2 * DC + 2 * DK) // HV
    col_g = (2 * DC + 2 * DK + DV) // DV
    col_ga, col_gb = z0 // D, z0 // D + 1

    new_state = None
    ns_shape = (B, DEPTH, 2, H, HK, HV)
    for l in range(DEPTH):
        mods = mods_all[l]
        w_main = jnp.concatenate([w_in[l, :, :z0], w_in[l, :, z0 + 2 * RANK:]], axis=1).astype(BF16)
        w_z = jnp.pad(w_in[l, :, z0:z0 + 2 * RANK], ((0, 0), (0, 128 - 2 * RANK))).astype(BF16)
        proj, z = _in_proj(x, norm_mix[l].reshape(1, D), mods, w_main, w_z, S)

        u_act = _conv_branch(proj, conv_w[l], conv_b[l], conv_norm[l], S, SEQ, DSEQ)

        w2pad = jnp.zeros((2, 128, DK), F32)
        w2pad = w2pad.at[0, :RANK].set(gla_gate_w2[l, 0]).at[1, RANK:2 * RANK].set(gla_gate_w2[l, 1])
        b2 = gla_gate_b[l].reshape(2, 1, DK)
        common = dict(H=H, HK=HK, HV=HV, col_q=col_q, col_k=col_k, col_v=col_v)
        o_norm, new_state = _gla(proj, z, w2pad, b2, gla_norm[l], row_off=0, n_seq=B, L=SEQ,
                                 ns_prev=new_state, ns_shape=ns_shape, ns_layer=l, **common)
        (o_norm,) = _gla(proj, z, w2pad, b2, gla_norm[l], row_off=S, n_seq=DB, L=DSEQ,
                         s0=state_gla, s0_layer=l, o_prev=o_norm, **common)

        merged = _branches(u_act, o_norm, proj, w_conv_out[l].astype(BF16), w_gla_out[l].astype(BF16),
                           col_g, col_ga, col_gb)
        x, h2, logits_t = _mix_out(merged, w_mix_out[l].astype(BF16), x, mods, norm_ffn[l].reshape(1, D),
                                   w_router[l].T, S)

        slot_p, gate_p = _route(logits_t, col0=0, n_groups=1, n_sets=B, L=SEQ, cap=cap_p)
        slot_s, gate_s = _route(logits_t, col0=1, n_groups=DB, n_sets=1, L=DSEQ, cap=cap_s)
        slot3 = jnp.concatenate([slot_p, slot_s], axis=1).reshape(E, 1, T)
        gate3 = jnp.concatenate([gate_p, gate_s], axis=1).reshape(E, 1, T)

        gathered, gslot = _gather(h2, slot3, gate3, S, nslot)
        act = _ffn_up(gathered, w_exp_gate, w_exp_up, l)
        yg = _ffn_down(act, w_exp_down, gslot, l)
        x = _scatter(slot3, yg, x, mods, S, nslot)

    nf = norm_final.reshape(1, D)
    y_prompt = _final_norm(x, nf, 0, B * SEQ).reshape(B, SEQ, D)
    y_sample = _final_norm(x, nf, S, DB * DSEQ).reshape(DB, DSEQ, D)
    return (y_prompt, y_sample, new_state)
```

```python
import functools

import jax
import jax.numpy as jnp
import numpy as np
from jax import lax
from jax.experimental import pallas as pl
from jax.experimental.pallas import tpu as pltpu

F32 = jnp.float32
BF16 = jnp.bfloat16
I32 = jnp.int32
HI = lax.Precision.HIGHEST

EPS = 1e-6
GRID_W = 64
POS_THETA = 10000.0
GLA_CHUNK = 64
GLA_STEP_CHUNKS = 4
GLA_TAU = 16.0
EC_CAPACITY_FACTOR = 2
N_MOD = 6
CONV_HALO = 16
SUBLANES = 8
LANES = 128
ROUTE_BLOCK = 256
VMEM_LIMIT = 56 * 1024 * 1024

NT_DIMS = (((1,), (1,)), ((), ()))
TN_DIMS = (((0,), (0,)), ((), ()))


def _cp(*sem):
    return pltpu.CompilerParams(dimension_semantics=sem, vmem_limit_bytes=VMEM_LIMIT)


def _sigmoid(x):
    return 1.0 / (1.0 + jnp.exp(-x))


def _silu(x):
    return x * _sigmoid(x)


def _rms(x):
    return x * lax.rsqrt(jnp.mean(x * x, axis=-1, keepdims=True) + EPS)


def _dot(a, b):
    return jnp.dot(a, b, preferred_element_type=F32)


def _dot_nt(a, b):
    return lax.dot_general(a, b, NT_DIMS, preferred_element_type=F32)


def _dot_tn(a, b):
    return lax.dot_general(a, b, TN_DIMS, preferred_element_type=F32)


def _split2(x):
    hi = x.astype(BF16)
    return hi, (x - hi.astype(F32)).astype(BF16)


def _split3(x):
    hi = x.astype(BF16)
    r = x - hi.astype(F32)
    mid = r.astype(BF16)
    return hi, mid, (r - mid.astype(F32)).astype(BF16)


def _embed_kernel(xp_ref, xs_ref, emb_ref, o_ref, *, nb_p):
    i = pl.program_id(0)

    @pl.when(i < nb_p)
    def _():
        o_ref[...] = xp_ref[...]

    @pl.when(i >= nb_p)
    def _():
        o_ref[...] = xs_ref[...] + emb_ref[...]


def _pos_table(n, d):
    rows = n // GRID_W
    rr, cc = jnp.meshgrid(jnp.arange(rows, dtype=F32), jnp.arange(GRID_W, dtype=F32), indexing="ij")
    rr = rr.reshape(-1, 1)
    cc = cc.reshape(-1, 1)
    quarter = d // 4
    freqs = POS_THETA ** (-jnp.arange(quarter, dtype=F32) / quarter)
    return jnp.concatenate([jnp.sin(rr * freqs), jnp.cos(rr * freqs), jnp.sin(cc * freqs), jnp.cos(cc * freqs)], axis=-1)


def _embed(xp2, xs2, emb, S):
    tp, d = xp2.shape
    ts = xs2.shape[0]
    bm = min(512, S)
    nb_p, nb_seq = tp // bm, S // bm
    return pl.pallas_call(
        functools.partial(_embed_kernel, nb_p=nb_p),
        out_shape=jax.ShapeDtypeStruct((tp + ts, d), F32),
        grid=((tp + ts) // bm,),
        in_specs=[
            pl.BlockSpec((bm, d), lambda i: (jnp.minimum(i, nb_p - 1), 0)),
            pl.BlockSpec((bm, d), lambda i: (jnp.maximum(i - nb_p, 0), 0)),
            pl.BlockSpec((bm, d), lambda i: (jnp.maximum(i - nb_p, 0) % nb_seq, 0)),
        ],
        out_specs=pl.BlockSpec((bm, d), lambda i: (i, 0)),
        compiler_params=_cp("arbitrary"),
        name="embed",
    )(xp2, xs2, emb)


def _ada_kernel(c_ref, w_ref, b_ref, o_ref):
    s = _silu(c_ref[...])
    o_ref[0] = jnp.dot(s, w_ref[0], precision=HI, preferred_element_type=F32) + b_ref[0]


def _adaln(cond8, w_ada, b_ada):
    depth, d, n = w_ada.shape
    bn = 1024
    return pl.pallas_call(
        _ada_kernel,
        out_shape=jax.ShapeDtypeStruct((depth, 8, n), F32),
        grid=(depth, n // bn),
        in_specs=[
            pl.BlockSpec((8, d), lambda l, j: (0, 0)),
            pl.BlockSpec((1, d, bn), lambda l, j: (l, 0, j)),
            pl.BlockSpec((1, 1, bn), lambda l, j: (l, 0, j)),
        ],
        out_specs=pl.BlockSpec((1, 8, bn), lambda l, j: (l, 0, j)),
        compiler_params=_cp("arbitrary", "arbitrary"),
        name="adaln",
    )(cond8, w_ada, b_ada.reshape(depth, 1, n))


def _in_proj_kernel(x_ref, nw_ref, sh_ref, sc_ref, wa_ref, wb_ref, wz_ref, o_ref, z_ref, h_scr, *, na):
    j = pl.program_id(1)

    @pl.when(j == 0)
    def _():
        h = (_rms(x_ref[...]) * nw_ref[...]) * (1.0 + sc_ref[...]) + sh_ref[...]
        hb = h.astype(BF16)
        h_scr[...] = hb
        z_ref[...] = _dot(hb, wz_ref[...])

    @pl.when(j < na)
    def _():
        o_ref[...] = _dot(h_scr[...], wa_ref[...]).astype(o_ref.dtype)

    @pl.when(j >= na)
    def _():
        o_ref[...] = _dot(h_scr[...], wb_ref[...]).astype(o_ref.dtype)


def _in_proj(x, norm_w, mods, w_a, w_b, w_z, S):
    t, d = x.shape
    bm, bn = min(1024, S), 1024
    na, nb = w_a.shape[1] // bn, w_b.shape[1] // bn
    seg = lambda i: (i * bm) // S
    return pl.pallas_call(
        functools.partial(_in_proj_kernel, na=na),
        out_shape=(jax.ShapeDtypeStruct((t, (na + nb) * bn), BF16), jax.ShapeDtypeStruct((t, LANES), F32)),
        grid=(t // bm, na + nb),
        in_specs=[
            pl.BlockSpec((bm, d), lambda i, j: (i, 0)),
            pl.BlockSpec((1, d), lambda i, j: (0, 0)),
            pl.BlockSpec((None, 1, d), lambda i, j: (seg(i) * N_MOD + 0, 0, 0)),
            pl.BlockSpec((None, 1, d), lambda i, j: (seg(i) * N_MOD + 1, 0, 0)),
            pl.BlockSpec((d, bn), lambda i, j: (0, jnp.minimum(j, na - 1))),
            pl.BlockSpec((d, bn), lambda i, j: (0, jnp.maximum(j - na, 0))),
            pl.BlockSpec((d, LANES), lambda i, j: (0, 0)),
        ],
        out_specs=(pl.BlockSpec((bm, bn), lambda i, j: (i, j)), pl.BlockSpec((bm, LANES), lambda i, j: (i, 0))),
        scratch_shapes=[pltpu.VMEM((bm, d), BF16)],
        compiler_params=_cp("arbitrary", "arbitrary"),
        name="in_proj",
    )(x, norm_w, mods, mods, w_a, w_b, w_z)


CONV_TR, CONV_TC = 64, 256


def _conv_kernel(uv, ug, uvp, ugp, uvn, ugn, cw_ref, cb_ref, cn_ref, o_ref, buf, cv, *, tb, S, Lp, Ls, K):
    row0 = pl.program_id(0) * tb
    pos = jnp.where(row0 < S, lax.rem(row0, Lp), lax.rem(row0 - S, Ls))
    seq_len = jnp.where(row0 < S, Lp, Ls)
    at_start = pos == 0
    at_end = pos + tb == seq_len
    c = uv.shape[1]
    pad = K // 2

    def glu(a, b):
        return a[...].astype(F32) * _sigmoid(b[...].astype(F32))

    rows = tb + 2 * CONV_HALO
    buf[0, 0:CONV_HALO, :] = jnp.where(at_start, 0.0, glu(uvp, ugp))
    buf[0, CONV_HALO:CONV_HALO + tb, :] = glu(uv, ug)
    buf[0, CONV_HALO + tb:, :] = jnp.where(at_end, 0.0, glu(uvn, ugn))
    for p in range(1, SUBLANES):
        buf[p, 0:rows - SUBLANES, :] = buf[0, p:p + rows - SUBLANES, :]

    for r0 in range(0, tb, CONV_TR):
        for c0 in range(0, c, CONV_TC):
            acc = jnp.broadcast_to(cb_ref[:, c0:c0 + CONV_TC], (CONV_TR, CONV_TC))
            for k in range(K):
                start = CONV_HALO - pad + k
                p = start % SUBLANES
                a = start - p + r0
                acc = acc + cw_ref[k:k + 1, c0:c0 + CONV_TC] * buf[p, a:a + CONV_TR, c0:c0 + CONV_TC]
            cv[r0:r0 + CONV_TR, c0:c0 + CONV_TC] = acc

    y = _rms(cv[...]) * cn_ref[...]
    o_ref[...] = _silu(y).astype(o_ref.dtype)


def _conv_branch(proj, conv_w, conv_b, conv_norm, S, Lp, Ls):
    t = proj.shape[0]
    k, c = conv_w.shape
    tb = 256
    nh = tb // CONV_HALO
    last = t // CONV_HALO - 1
    main = lambda col: pl.BlockSpec((tb, c), lambda i: (i, col))
    prev = lambda col: pl.BlockSpec((CONV_HALO, c), lambda i: (jnp.maximum(i * nh - 1, 0), col))
    nxt = lambda col: pl.BlockSpec((CONV_HALO, c), lambda i: (jnp.minimum((i + 1) * nh, last), col))
    return pl.pallas_call(
        functools.partial(_conv_kernel, tb=tb, S=S, Lp=Lp, Ls=Ls, K=k),
        out_shape=jax.ShapeDtypeStruct((t, c), BF16),
        grid=(t // tb,),
        in_specs=[main(0), main(1), prev(0), prev(1), nxt(0), nxt(1),
                  pl.BlockSpec((k, c), lambda i: (0, 0)),
                  pl.BlockSpec((1, c), lambda i: (0, 0)),
                  pl.BlockSpec((1, c), lambda i: (0, 0))],
        out_specs=pl.BlockSpec((tb, c), lambda i: (i, 0)),
        scratch_shapes=[pltpu.VMEM((SUBLANES, tb + 2 * CONV_HALO, c), F32), pltpu.VMEM((tb, c), F32)],
        compiler_params=_cp("arbitrary"),
        name="conv_branch",
    )(proj, proj, proj, proj, proj, proj, conv_w, conv_b.reshape(1, c), conv_norm.reshape(1, c))


def _log_sigmoid(x):
    return jnp.minimum(x, 0.0) - jnp.log(1.0 + jnp.exp(-jnp.abs(x)))


def _gla_masks():
    C, n = GLA_CHUNK, GLA_STEP_CHUNKS
    R = C * n
    r = np.arange(R)[:, None]
    s = np.arange(R)[None, :]
    tris, codes = [], []
    for reverse in (False, True):
        sr, sc = r // C, s // C
        before_eq = s <= r
        if reverse:
            sr, sc, before_eq = n - 1 - sr, n - 1 - sc, s >= r
        sr, sc = np.broadcast_to(sr, (R, R)), np.broadcast_to(sc, (R, R))
        code = np.full((R, R), 3, np.int32)
        code[(sr >= 2) & (sc <= 1)] = 2
        code[(sr == sc + 1) & (sr % 2 == 1)] = 1
        code[(sr == sc) & before_eq] = 0
        codes.append(code)
        tris.append((code == 0).astype(np.float32))
    return jnp.asarray(np.stack(tris), BF16), jnp.asarray(np.stack(codes), I32)


def _gla_dir(q_ref, k_ref, v_ref, z_ref, w2h, w2l, b2, tri, code, st_ref, reverse, scale):
    C, n = GLA_CHUNK, GLA_STEP_CHUNKS
    R = C * n
    zh, zl = _split2(z_ref[...])
    pre = _dot(zl, w2h) + _dot(zh, w2l) + _dot(zh, w2h) + b2
    la = _log_sigmoid(pre) / GLA_TAU
    lh, lm, ll = _split3(la)
    a1 = _dot(tri, ll) + _dot(tri, lm) + _dot(tri, lh)

    last = 0 if reverse else C - 1
    order = list(range(n))[::-1] if reverse else list(range(n))
    cend = [a1[c * C + last:c * C + last + 1, :] for c in range(n)]
    c0, c1, c2, c3 = (cend[c] for c in order)
    e_before = {order[1]: c0, order[2]: c0 + c1, order[3]: c0 + c1 + c2}
    e_after = {order[0]: c1 + c2 + c3, order[1]: c2 + c3, order[2]: c3}
    etot = c0 + c1 + c2 + c3

    q_in, k_in, k_end, q_st, k_st, q_mid, k_mid = ([] for _ in range(7))
    for c in range(n):
        rows = slice(c * C, (c + 1) * C)
        a = a1[rows, :]
        qi = (q_ref[rows, :].astype(F32) * scale) * jnp.exp(a)
        k32 = k_ref[rows, :].astype(F32)
        ke = k32 * jnp.exp(cend[c] - a)
        q_in.append(qi)
        k_in.append(k32 * jnp.exp(-a))
        k_end.append(ke)
        q_st.append(qi * jnp.exp(e_before[c]) if c in e_before else qi)
        k_st.append(ke * jnp.exp(e_after[c]) if c in e_after else ke)
        q_mid.append(qi * jnp.exp(c2) if c == order[3] else qi)
        k_mid.append(ke * jnp.exp(c1) if c == order[0] else ke)
    cat = lambda xs: jnp.concatenate([x.astype(BF16) for x in xs], axis=0)
    q_in, q_st, k_st, q_mid, k_mid = cat(q_in), cat(q_st), cat(k_st), cat(q_mid), cat(k_mid)
    k_both = cat(k_in + k_end)

    s01 = _dot_nt(q_in, k_both)
    s2 = _dot_nt(q_mid, k_mid)
    p = jnp.where(code == 0, s01[:, :R], jnp.where(code == 1, s01[:, R:], jnp.where(code == 2, s2, 0.0)))
    st = st_ref[...]
    v = v_ref[...]
    o = _dot(p.astype(BF16), v) + _dot_nt(q_st, st.astype(BF16))
    st_ref[...] = st * jnp.exp(etot) + _dot_tn(v, k_st)
    return o


def _gla_kernel(*refs, nsteps, scale, has_init, n_aliased, has_final):
    qf, kf, vf, zf, qb, kb, vb, zb, w2h_ref, w2l_ref, b2_ref, gn_ref, g_ref, tri_ref, code_ref = refs[:15]
    rest = list(refs[15:])
    s0_ref = rest.pop(0) if has_init else None
    rest = rest[n_aliased:]
    o_ref = rest.pop(0)
    sfin_ref = rest.pop(0) if has_final else None
    o_acc, stf, stb = rest

    c = pl.program_id(2)
    R = GLA_CHUNK * GLA_STEP_CHUNKS

    @pl.when(c == 0)
    def _():
        o_acc[...] = jnp.zeros_like(o_acc)
        if has_init:
            stf[...] = s0_ref[0].T
            stb[...] = s0_ref[1].T
        else:
            stf[...] = jnp.zeros_like(stf)
            stb[...] = jnp.zeros_like(stb)

    of = _gla_dir(qf, kf, vf, zf, w2h_ref[0], w2l_ref[0], b2_ref[0], tri_ref[0], code_ref[0], stf, False, scale)
    ob = _gla_dir(qb, kb, vb, zb, w2h_ref[1], w2l_ref[1], b2_ref[1], tri_ref[1], code_ref[1], stb, True, scale)
    o_acc[pl.ds(pl.multiple_of(c * R, R), R), :] += of
    o_acc[pl.ds(pl.multiple_of((nsteps - 1 - c) * R, R), R), :] += ob

    @pl.when(c == nsteps - 1)
    def _():
        o = _rms(o_acc[...]) * gn_ref[...]
        o_ref[...] = (o * _silu(g_ref[...].astype(F32))).astype(o_ref.dtype)
        if has_final:
            sfin_ref[0] = stf[...].T
            sfin_ref[1] = stb[...].T


def _gla(proj, z, w2h, w2l, b2, gla_norm, tri, code, *, row_off, n_seq, L, H, HK, HV, col_q, col_k, col_v, col_g,
         s0=None, s0_layer=0, o_prev=None, ns_prev=None, ns_shape=None, ns_layer=0):
    t = proj.shape[0]
    R = GLA_CHUNK * GLA_STEP_CHUNKS
    nsteps = L // R
    rb0 = row_off // R
    has_init = s0 is not None
    has_final = ns_shape is not None
    fwd = lambda b, h, c: rb0 + b * nsteps + c
    bwd = lambda b, h, c: rb0 + b * nsteps + nsteps - 1 - c
    whole = lambda shape: pl.BlockSpec(shape, lambda b, h, c: (0,) * len(shape))

    def specs(rowmap):
        return [pl.BlockSpec((R, HK), lambda b, h, c: (rowmap(b, h, c), col_q + h)),
                pl.BlockSpec((R, HK), lambda b, h, c: (rowmap(b, h, c), col_k + h)),
                pl.BlockSpec((R, HV), lambda b, h, c: (rowmap(b, h, c), col_v + h)),
                pl.BlockSpec((R, LANES), lambda b, h, c: (rowmap(b, h, c), 0))]

    in_specs = specs(fwd) + specs(bwd) + [
        pl.BlockSpec((2, LANES, HK), lambda b, h, c: (0, 0, h)),
        pl.BlockSpec((2, LANES, HK), lambda b, h, c: (0, 0, h)),
        pl.BlockSpec((2, 1, HK), lambda b, h, c: (0, 0, h)),
        whole((1, HV)),
        pl.BlockSpec((L, HV), lambda b, h, c: (row_off // L + b, col_g + h)),
        whole((2, R, R)),
        whole((2, R, R)),
    ]
    args = [proj, proj, proj, z, proj, proj, proj, z, w2h, w2l, b2, gla_norm.reshape(1, HV), proj, tri, code]
    if has_init:
        in_specs.append(pl.BlockSpec((None, None, 2, None, HK, HV), lambda b, h, c: (b, s0_layer, 0, h, 0, 0)))
        args.append(s0)
    aliases = {}
    out_shape = [jax.ShapeDtypeStruct((t, H * HV), BF16)]
    out_specs = [pl.BlockSpec((L, HV), lambda b, h, c: (row_off // L + b, h))]
    if o_prev is not None:
        aliases[len(args)] = 0
        in_specs.append(pl.BlockSpec(memory_space=pl.ANY))
        args.append(o_prev)
    if has_final:
        out_shape.append(jax.ShapeDtypeStruct(ns_shape, F32))
        out_specs.append(pl.BlockSpec((None, None, 2, None, HK, HV), lambda b, h, c: (b, ns_layer, 0, h, 0, 0)))
        if ns_prev is not None:
            aliases[len(args)] = 1
            in_specs.append(pl.BlockSpec(memory_space=pl.ANY))
            args.append(ns_prev)
    return pl.pallas_call(
        functools.partial(_gla_kernel, nsteps=nsteps, scale=float(HK) ** -0.5,
                          has_init=has_init, n_aliased=len(aliases), has_final=has_final),
        out_shape=tuple(out_shape),
        grid=(n_seq, H, nsteps),
        in_specs=in_specs,
        out_specs=tuple(out_specs),
        scratch_shapes=[pltpu.VMEM((L, HV), F32), pltpu.VMEM((HV, HK), F32), pltpu.VMEM((HV, HK), F32)],
        input_output_aliases=aliases,
        compiler_params=_cp("arbitrary", "arbitrary", "arbitrary"),
        name="gla_init" if has_init else "gla_zero",
    )(*args)


def _branch_kernel(u_ref, og_ref, ga_ref, gb_ref, wc_ref, wg_ref, m_ref):
    a = _dot(u_ref[...], wc_ref[...])
    b = _dot(og_ref[...], wg_ref[...])
    m = _sigmoid(ga_ref[...].astype(F32)) * a + _sigmoid(gb_ref[...].astype(F32)) * b
    m_ref[...] = m.astype(m_ref.dtype)


def _branches(u_act, og, proj, w_conv_out, w_gla_out, col_ga, col_gb):
    t, c = u_act.shape
    d = w_conv_out.shape[1]
    dv = w_gla_out.shape[0]
    bm = 512
    return pl.pallas_call(
        _branch_kernel,
        out_shape=jax.ShapeDtypeStruct((t, d), BF16),
        grid=(t // bm,),
        in_specs=[
            pl.BlockSpec((bm, c), lambda i: (i, 0)),
            pl.BlockSpec((bm, dv), lambda i: (i, 0)),
            pl.BlockSpec((bm, d), lambda i: (i, col_ga)),
            pl.BlockSpec((bm, d), lambda i: (i, col_gb)),
            pl.BlockSpec((c, d), lambda i: (0, 0)),
            pl.BlockSpec((dv, d), lambda i: (0, 0)),
        ],
        out_specs=pl.BlockSpec((bm, d), lambda i: (i, 0)),
        compiler_params=_cp("arbitrary"),
        name="branches",
    )(u_act, og, proj, proj, w_conv_out, w_gla_out)


def _mix_out_kernel(m_ref, w_ref, x_ref, g1_ref, nw_ref, sh_ref, sc_ref, wrh_ref, wrl_ref, xo_ref, h_ref, lg_ref):
    x = x_ref[...] + g1_ref[...] * _dot(m_ref[...], w_ref[...])
    xo_ref[...] = x
    h = (_rms(x) * nw_ref[...]) * (1.0 + sc_ref[...]) + sh_ref[...]
    hh, hl = _split2(h)
    h_ref[...] = hh
    lg_ref[...] = _dot_nt(wrl_ref[...], hh) + _dot_nt(wrh_ref[...], hl) + _dot_nt(wrh_ref[...], hh)


def _mix_out(merged, w_mix, x, mods, norm_w, wr_hi, wr_lo, S):
    t, d = x.shape
    e = wr_hi.shape[0]
    bm = 512
    seg = lambda i: (i * bm) // S
    mod = lambda which: pl.BlockSpec((None, 1, d), lambda i: (seg(i) * N_MOD + which, 0, 0))
    return pl.pallas_call(
        _mix_out_kernel,
        out_shape=(jax.ShapeDtypeStruct((t, d), F32), jax.ShapeDtypeStruct((t, d), BF16),
                   jax.ShapeDtypeStruct((e, t), F32)),
        grid=(t // bm,),
        in_specs=[
            pl.BlockSpec((bm, d), lambda i: (i, 0)),
            pl.BlockSpec((d, d), lambda i: (0, 0)),
            pl.BlockSpec((bm, d), lambda i: (i, 0)),
            mod(2),
            pl.BlockSpec((1, d), lambda i: (0, 0)),
            mod(3), mod(4),
            pl.BlockSpec((e, d), lambda i: (0, 0)),
            pl.BlockSpec((e, d), lambda i: (0, 0)),
        ],
        out_specs=(pl.BlockSpec((bm, d), lambda i: (i, 0)), pl.BlockSpec((bm, d), lambda i: (i, 0)),
                   pl.BlockSpec((e, bm), lambda i: (0, i))),
        compiler_params=_cp("arbitrary"),
        name="mix_out",
    )(merged, w_mix, x, mods, norm_w, mods, mods, wr_hi, wr_lo)


def _route_kernel(lg_ref, slot_ref, gate_ref, cnt_ref, *, n_sets, L, cap):
    lg = lg_ref[...]
    e = lg.shape[0]
    ex = jnp.exp(lg - jnp.max(lg, axis=0, keepdims=True))
    aff = ex / jnp.sum(ex, axis=0, keepdims=True)
    bits = pltpu.bitcast(aff, I32)
    sets = [bits[:, s * L:(s + 1) * L] for s in range(n_sets)]

    def body(it, vs):
        bit = jnp.left_shift(jnp.int32(1), 30 - it)
        out = []
        for a, v in zip(sets, vs):
            cand = v | bit
            cnt = jnp.sum(jnp.where(a >= cand, 1.0, 0.0), axis=1, keepdims=True)
            out.append(jnp.where(cnt >= cap, cand, v))
        return tuple(out)

    thr = lax.fori_loop(0, 31, body, tuple(jnp.zeros((e, 1), I32) for _ in range(n_sets)))

    pb = ROUTE_BLOCK
    r = lax.broadcasted_iota(I32, (pb, pb), 0)
    c = lax.broadcasted_iota(I32, (pb, pb), 1)
    tri = jnp.where(r <= c, 1.0, 0.0).astype(BF16)
    cnt_ref[...] = jnp.full(cnt_ref.shape, n_sets * cap, I32)
    for s in range(n_sets):
        a, v = sets[s], thr[s]
        n_gt = jnp.sum(jnp.where(a > v, 1.0, 0.0), axis=1, keepdims=True)
        need = cap - n_gt
        carry_eq = jnp.zeros((e, 1), F32)
        carry_sel = jnp.zeros((e, 1), F32)
        for b0 in range(0, L, pb):
            blk = (s * L + b0) // pb
            cnt_ref[:, blk:blk + 1] = carry_sel.astype(I32) + s * cap
            ab = a[:, b0:b0 + pb]
            eq = jnp.where(ab == v, 1.0, 0.0)
            rank_eq = _dot(eq.astype(BF16), tri) - eq + carry_eq
            sel = jnp.where(ab > v, 1.0, jnp.where(rank_eq < need, eq, 0.0))
            rank_sel = _dot(sel.astype(BF16), tri) - sel + carry_sel
            carry_eq = carry_eq + jnp.sum(eq, axis=1, keepdims=True)
            carry_sel = carry_sel + jnp.sum(sel, axis=1, keepdims=True)
            cols = slice(s * L + b0, s * L + b0 + pb)
            slot_ref[:, cols] = jnp.where(sel > 0.0, rank_sel.astype(I32) + s * cap, -1)
            gate_ref[:, cols] = jnp.where(sel > 0.0, aff[:, cols], 0.0)


def _route(logits_t, *, col0, n_groups, n_sets, L, cap):
    e = logits_t.shape[0]
    n = n_sets * L
    assert n // ROUTE_BLOCK < LANES
    return pl.pallas_call(
        functools.partial(_route_kernel, n_sets=n_sets, L=L, cap=cap),
        out_shape=(jax.ShapeDtypeStruct((e, n_groups * n), I32), jax.ShapeDtypeStruct((e, n_groups * n), F32),
                   jax.ShapeDtypeStruct((e, n_groups * LANES), I32)),
        grid=(n_groups,),
        in_specs=[pl.BlockSpec((e, n), lambda g: (0, col0 + g))],
        out_specs=(pl.BlockSpec((e, n), lambda g: (0, g)), pl.BlockSpec((e, n), lambda g: (0, g)),
                   pl.BlockSpec((e, LANES), lambda g: (0, g))),
        compiler_params=_cp("arbitrary"),
        name="route",
    )(logits_t)


GATHER_TOKENS = 512
GATHER_SLOTS = 128
SCATTER_TOKENS = 512
SCATTER_SLOTS = 256


def _gather_kernel(cnt_ref, h_ref, slot_ref, gate_ref, g_ref, gs_ref, acc, gs_acc, *, nslot, n_exp, n_cnt):
    s_tok = h_ref.shape[0]
    tk = min(GATHER_TOKENS, s_tok)
    sb = min(GATHER_SLOTS, nslot)
    base = (pl.program_id(0) * n_exp + pl.program_id(1)) * n_cnt
    acc[...] = jnp.zeros_like(acc)
    gs_acc[...] = jnp.zeros_like(gs_acc)
    for c in range(s_tok // tk):
        lo = cnt_ref[base + c * (tk // ROUTE_BLOCK)]
        hi = cnt_ref[base + (c + 1) * (tk // ROUTE_BLOCK)]

        def body(j, carry, c=c):
            off = pl.multiple_of(j * sb, sb)
            hit = slot_ref[:, c * tk:(c + 1) * tk] == lax.broadcasted_iota(I32, (sb, tk), 0) + off
            onehot = jnp.where(hit, 1.0, 0.0).astype(BF16)
            acc[pl.ds(off, sb), :] += _dot(onehot, h_ref[c * tk:(c + 1) * tk, :])
            gs_acc[pl.ds(off, sb), :] += jnp.sum(jnp.where(hit, gate_ref[:, c * tk:(c + 1) * tk], 0.0),
                                                 axis=1, keepdims=True)
            return carry

        lax.fori_loop(lo // sb, (hi + sb - 1) // sb, body, 0)
    g_ref[...] = acc[...].astype(g_ref.dtype)
    gs_ref[...] = gs_acc[...]


def _gather(cnt, h2, slot3, gate3, S, nslot, n_cnt):
    t, d = h2.shape
    e = slot3.shape[0]
    nseg = t // S
    return pl.pallas_call(
        functools.partial(_gather_kernel, nslot=nslot, n_exp=e, n_cnt=n_cnt),
        out_shape=(jax.ShapeDtypeStruct((e, nseg * nslot, d), BF16), jax.ShapeDtypeStruct((e, nseg * nslot, 1), F32)),
        grid_spec=pltpu.PrefetchScalarGridSpec(
            num_scalar_prefetch=1,
            grid=(nseg, e),
            in_specs=[
                pl.BlockSpec((S, d), lambda s, x, cnt: (s, 0)),
                pl.BlockSpec((None, 1, S), lambda s, x, cnt: (x, 0, s)),
                pl.BlockSpec((None, 1, S), lambda s, x, cnt: (x, 0, s)),
            ],
            out_specs=(pl.BlockSpec((None, nslot, d), lambda s, x, cnt: (x, s, 0)),
                       pl.BlockSpec((None, nslot, 1), lambda s, x, cnt: (x, s, 0))),
            scratch_shapes=[pltpu.VMEM((nslot, d), F32), pltpu.VMEM((nslot, 1), F32)],
        ),
        compiler_params=_cp("arbitrary", "arbitrary"),
        name="moe_gather",
    )(cnt, h2, slot3, gate3)


def _ffn_up_kernel(g_ref, wg_ref, wu_ref, a_ref):
    g = g_ref[...]
    a = _dot(g, wg_ref[...].astype(BF16))
    b = _dot(g, wu_ref[...].astype(BF16))
    a_ref[...] = (_silu(a) * b).astype(a_ref.dtype)


def _ffn_up(gathered, w_gate, w_up, layer):
    e, m, d = gathered.shape
    f = w_gate.shape[3]
    bf = 512
    return pl.pallas_call(
        _ffn_up_kernel,
        out_shape=jax.ShapeDtypeStruct((e, m, f), BF16),
        grid=(e, f // bf),
        in_specs=[
            pl.BlockSpec((None, m, d), lambda x, j: (x, 0, 0)),
            pl.BlockSpec((None, None, d, bf), lambda x, j: (layer, x, 0, j)),
            pl.BlockSpec((None, None, d, bf), lambda x, j: (layer, x, 0, j)),
        ],
        out_specs=pl.BlockSpec((None, m, bf), lambda x, j: (x, 0, j)),
        compiler_params=_cp("arbitrary", "arbitrary"),
        name="ffn_up",
    )(gathered, w_gate, w_up)


def _ffn_down_kernel(a_ref, wd_ref, gs_ref, y_ref):
    y = _dot(a_ref[...], wd_ref[...].astype(BF16))
    y_ref[...] = (y * gs_ref[...]).astype(y_ref.dtype)


def _ffn_down(act, w_down, gslot, layer):
    e, m, f = act.shape
    d = w_down.shape[3]
    bn = 512
    return pl.pallas_call(
        _ffn_down_kernel,
        out_shape=jax.ShapeDtypeStruct((e, m, d), BF16),
        grid=(e, d // bn),
        in_specs=[
            pl.BlockSpec((None, m, f), lambda x, j: (x, 0, 0)),
            pl.BlockSpec((None, None, f, bn), lambda x, j: (layer, x, 0, j)),
            pl.BlockSpec((None, m, 1), lambda x, j: (x, 0, 0)),
        ],
        out_specs=pl.BlockSpec((None, m, bn), lambda x, j: (x, 0, j)),
        compiler_params=_cp("arbitrary", "arbitrary"),
        name="ffn_down",
    )(act, w_down, gslot)


def _scatter_kernel(cnt_ref, slot_ref, y_ref, x_ref, g2_ref, o_ref, *, nslot, n_exp, n_cnt, S):
    i = pl.program_id(0)
    x = pl.program_id(1)
    tb = o_ref.shape[0]
    sb = min(SCATTER_SLOTS, nslot)
    seg = (i * tb) // S
    blk = (i * tb - seg * S) // ROUTE_BLOCK
    base = (seg * n_exp + x) * n_cnt + blk
    lo = cnt_ref[base]
    hi = cnt_ref[base + tb // ROUTE_BLOCK]

    @pl.when(x == 0)
    def _():
        o_ref[...] = jnp.zeros_like(o_ref)

    def body(j, carry):
        off = pl.multiple_of(j * sb, sb)
        hit = slot_ref[...] == lax.broadcasted_iota(I32, (sb, tb), 0) + off
        onehot = jnp.where(hit, 1.0, 0.0).astype(BF16)
        o_ref[...] += _dot_tn(onehot, y_ref[pl.ds(off, sb), :])
        return carry

    lax.fori_loop(lo // sb, (hi + sb - 1) // sb, body, 0)

    @pl.when(x == pl.num_programs(1) - 1)
    def _():
        o_ref[...] = x_ref[...] + g2_ref[...] * o_ref[...]


def _scatter(cnt, slot3, yg, x, mods, S, nslot, n_cnt):
    t, d = x.shape
    e = slot3.shape[0]
    tb = min(SCATTER_TOKENS, S)
    seg = lambda i: (i * tb) // S
    return pl.pallas_call(
        functools.partial(_scatter_kernel, nslot=nslot, n_exp=e, n_cnt=n_cnt, S=S),
        out_shape=jax.ShapeDtypeStruct((t, d), F32),
        grid_spec=pltpu.PrefetchScalarGridSpec(
            num_scalar_prefetch=1,
            grid=(t // tb, e),
            in_specs=[
                pl.BlockSpec((None, 1, tb), lambda i, x, cnt: (x, 0, i)),
                pl.BlockSpec((None, nslot, d), lambda i, x, cnt: (x, seg(i), 0)),
                pl.BlockSpec((tb, d), lambda i, x, cnt: (i, 0)),
                pl.BlockSpec((None, 1, d), lambda i, x, cnt: (seg(i) * N_MOD + 5, 0, 0)),
            ],
            out_specs=pl.BlockSpec((tb, d), lambda i, x, cnt: (i, 0)),
        ),
        compiler_params=_cp("arbitrary", "arbitrary"),
        name="moe_scatter",
    )(cnt, slot3, yg, x, mods)


def _final_norm_kernel(x_ref, w_ref, o_ref):
    o_ref[...] = _rms(x_ref[...]) * w_ref[...]


def _final_norm(x, w, row_off, rows):
    d = x.shape[1]
    bm = 512
    off = row_off // bm
    return pl.pallas_call(
        _final_norm_kernel,
        out_shape=jax.ShapeDtypeStruct((rows, d), F32),
        grid=(rows // bm,),
        in_specs=[pl.BlockSpec((bm, d), lambda i: (off + i, 0)), pl.BlockSpec((1, d), lambda i: (0, 0))],
        out_specs=pl.BlockSpec((bm, d), lambda i: (i, 0)),
        compiler_params=_cp("arbitrary"),
        name="final_norm",
    )(x, w)


def kernel(x_prompt, x_sample, state_gla, c, c_ctx, w_in, conv_w, conv_b, conv_norm, w_conv_out, gla_gate_w2, gla_gate_b, gla_norm, w_gla_out, w_mix_out, w_ada, b_ada, norm_mix, norm_ffn, w_router, w_exp_gate, w_exp_up, w_exp_down, norm_final):
    B, SEQ, D = x_prompt.shape
    DB, DSEQ, _ = x_sample.shape
    DEPTH = w_in.shape[0]
    H, HK, HV = state_gla.shape[3:]
    DC = conv_w.shape[2]
    DK, DV = H * HK, H * HV
    RANK = gla_gate_w2.shape[2]
    E = w_router.shape[2]
    S = DSEQ
    R = GLA_CHUNK * GLA_STEP_CHUNKS
    assert B * SEQ == S, "prompt tokens must fill exactly one segment"
    assert 2 * RANK <= LANES and conv_w.shape[1] // 2 <= CONV_HALO
    assert SEQ % R == 0 and DSEQ % R == 0 and SEQ % ROUTE_BLOCK == 0 and S % GATHER_TOKENS == 0
    nseg = 1 + DB
    assert nseg <= 8
    T = nseg * S
    cap_p = (EC_CAPACITY_FACTOR * SEQ) // E
    cap_s = (EC_CAPACITY_FACTOR * DSEQ) // E
    nslot = B * cap_p
    assert nslot == cap_s
    n_cnt = S // ROUTE_BLOCK + 1

    x = _embed(x_prompt.reshape(B * SEQ, D), x_sample.reshape(DB * DSEQ, D), _pos_table(DSEQ, D), S)

    cond8 = jnp.zeros((8, D), F32).at[0].set(c_ctx).at[1:1 + DB].set(c)
    mods_all = _adaln(cond8, w_ada, b_ada).reshape(DEPTH, 8 * N_MOD, 1, D)

    z0 = 2 * DC + 2 * DK + 2 * DV
    col_q, col_k = (2 * DC) // HK, (2 * DC + DK) // HK
    col_v = (2 * DC + 2 * DK) // HV
    col_g = (2 * DC + 2 * DK + DV) // HV
    col_ga, col_gb = z0 // D, z0 // D + 1
    tri, code = _gla_masks()

    new_state = None
    ns_shape = (B, DEPTH, 2, H, HK, HV)
    for l in range(DEPTH):
        mods = mods_all[l]
        w_a = w_in[l, :, :z0].astype(BF16)
        w_b = w_in[l, :, z0 + 2 * RANK:].astype(BF16)
        w_z = jnp.pad(w_in[l, :, z0:z0 + 2 * RANK], ((0, 0), (0, LANES - 2 * RANK))).astype(BF16)
        proj, z = _in_proj(x, norm_mix[l].reshape(1, D), mods, w_a, w_b, w_z, S)

        u_act = _conv_branch(proj, conv_w[l], conv_b[l], conv_norm[l], S, SEQ, DSEQ)

        w2pad = jnp.zeros((2, LANES, DK), F32)
        w2pad = w2pad.at[0, :RANK].set(gla_gate_w2[l, 0]).at[1, RANK:2 * RANK].set(gla_gate_w2[l, 1])
        w2h = w2pad.astype(BF16)
        w2l = (w2pad - w2h.astype(F32)).astype(BF16)
        b2 = gla_gate_b[l].reshape(2, 1, DK)
        common = dict(H=H, HK=HK, HV=HV, col_q=col_q, col_k=col_k, col_v=col_v, col_g=col_g)
        og, new_state = _gla(proj, z, w2h, w2l, b2, gla_norm[l], tri, code, row_off=0, n_seq=B, L=SEQ,
                             ns_prev=new_state, ns_shape=ns_shape, ns_layer=l, **common)
        (og,) = _gla(proj, z, w2h, w2l, b2, gla_norm[l], tri, code, row_off=S, n_seq=DB, L=DSEQ,
                     s0=state_gla, s0_layer=l, o_prev=og, **common)

        merged = _branches(u_act, og, proj, w_conv_out[l].astype(BF16), w_gla_out[l].astype(BF16), col_ga, col_gb)
        wr_t = w_router[l].T
        wr_hi = wr_t.astype(BF16)
        wr_lo = (wr_t - wr_hi.astype(F32)).astype(BF16)
        x, h2, logits_t = _mix_out(merged, w_mix_out[l].astype(BF16), x, mods, norm_ffn[l].reshape(1, D),
                                   wr_hi, wr_lo, S)

        slot_p, gate_p, cnt_p = _route(logits_t, col0=0, n_groups=1, n_sets=B, L=SEQ, cap=cap_p)
        slot_s, gate_s, cnt_s = _route(logits_t, col0=1, n_groups=DB, n_sets=1, L=DSEQ, cap=cap_s)
        slot3 = jnp.concatenate([slot_p, slot_s], axis=1).reshape(E, 1, T)
        gate3 = jnp.concatenate([gate_p, gate_s], axis=1).reshape(E, 1, T)
        cnt = jnp.concatenate([cnt_p, cnt_s], axis=1).reshape(E, nseg, LANES)[:, :, :n_cnt]
        cnt = cnt.transpose(1, 0, 2).reshape(nseg * E * n_cnt)

        gathered, gslot = _gather(cnt, h2, slot3, gate3, S, nslot, n_cnt)
        act = _ffn_up(gathered, w_exp_gate, w_exp_up, l)
        yg = _ffn_down(act, w_exp_down, gslot, l)
        x = _scatter(cnt, slot3, yg, x, mods, S, nslot, n_cnt)

    nf = norm_final.reshape(1, D)
    y_prompt = _final_norm(x, nf, 0, B * SEQ).reshape(B, SEQ, D)
    y_sample = _final_norm(x, nf, S, DB * DSEQ).reshape(DB, DSEQ, D)
    return (y_prompt, y_sample, new_state)
```

```python
import functools

import jax
import jax.numpy as jnp
import numpy as np
from jax import lax
from jax.experimental import pallas as pl
from jax.experimental.pallas import tpu as pltpu

F32 = jnp.float32
BF16 = jnp.bfloat16
I32 = jnp.int32
HI = lax.Precision.HIGHEST

EPS = 1e-6
GRID_W = 64
POS_THETA = 10000.0
GLA_CHUNK = 64
GLA_STEP_CHUNKS = 4
GLA_TAU = 16.0
EC_CAPACITY_FACTOR = 2
N_MOD = 6
CONV_HALO = 16
SUBLANES = 8
LANES = 128
ROUTE_BLOCK = 256
VMEM_LIMIT = 56 * 1024 * 1024

NT_DIMS = (((1,), (1,)), ((), ()))
TN_DIMS = (((0,), (0,)), ((), ()))


def _cp(*sem):
    return pltpu.CompilerParams(dimension_semantics=sem, vmem_limit_bytes=VMEM_LIMIT)


def _sigmoid(x):
    return 1.0 / (1.0 + jnp.exp(-x))


def _silu(x):
    return x * _sigmoid(x)


def _rms(x):
    return x * lax.rsqrt(jnp.mean(x * x, axis=-1, keepdims=True) + EPS)


def _dot(a, b):
    return jnp.dot(a, b, preferred_element_type=F32)


def _dot_nt(a, b):
    return lax.dot_general(a, b, NT_DIMS, preferred_element_type=F32)


def _dot_tn(a, b):
    return lax.dot_general(a, b, TN_DIMS, preferred_element_type=F32)


def _split2(x):
    hi = x.astype(BF16)
    return hi, (x - hi.astype(F32)).astype(BF16)


def _split3(x):
    hi = x.astype(BF16)
    r = x - hi.astype(F32)
    mid = r.astype(BF16)
    return hi, mid, (r - mid.astype(F32)).astype(BF16)


def _embed_kernel(xp_ref, xs_ref, emb_ref, o_ref, *, nb_p):
    i = pl.program_id(0)

    @pl.when(i < nb_p)
    def _():
        o_ref[...] = xp_ref[...]

    @pl.when(i >= nb_p)
    def _():
        o_ref[...] = xs_ref[...] + emb_ref[...]


def _pos_table(n, d):
    rows = n // GRID_W
    rr, cc = jnp.meshgrid(jnp.arange(rows, dtype=F32), jnp.arange(GRID_W, dtype=F32), indexing="ij")
    rr = rr.reshape(-1, 1)
    cc = cc.reshape(-1, 1)
    quarter = d // 4
    freqs = POS_THETA ** (-jnp.arange(quarter, dtype=F32) / quarter)
    return jnp.concatenate([jnp.sin(rr * freqs), jnp.cos(rr * freqs), jnp.sin(cc * freqs), jnp.cos(cc * freqs)], axis=-1)


def _embed(xp2, xs2, emb, S):
    tp, d = xp2.shape
    ts = xs2.shape[0]
    bm = min(512, S)
    nb_p, nb_seq = tp // bm, S // bm
    return pl.pallas_call(
        functools.partial(_embed_kernel, nb_p=nb_p),
        out_shape=jax.ShapeDtypeStruct((tp + ts, d), F32),
        grid=((tp + ts) // bm,),
        in_specs=[
            pl.BlockSpec((bm, d), lambda i: (jnp.minimum(i, nb_p - 1), 0)),
            pl.BlockSpec((bm, d), lambda i: (jnp.maximum(i - nb_p, 0), 0)),
            pl.BlockSpec((bm, d), lambda i: (jnp.maximum(i - nb_p, 0) % nb_seq, 0)),
        ],
        out_specs=pl.BlockSpec((bm, d), lambda i: (i, 0)),
        compiler_params=_cp("arbitrary"),
        name="embed",
    )(xp2, xs2, emb)


def _ada_kernel(c_ref, w_ref, b_ref, o_ref):
    s = _silu(c_ref[...])
    o_ref[0] = jnp.dot(s, w_ref[0], precision=HI, preferred_element_type=F32) + b_ref[0]


def _adaln(cond8, w_ada, b_ada):
    depth, d, n = w_ada.shape
    bn = 1024
    return pl.pallas_call(
        _ada_kernel,
        out_shape=jax.ShapeDtypeStruct((depth, 8, n), F32),
        grid=(depth, n // bn),
        in_specs=[
            pl.BlockSpec((8, d), lambda l, j: (0, 0)),
            pl.BlockSpec((1, d, bn), lambda l, j: (l, 0, j)),
            pl.BlockSpec((1, 1, bn), lambda l, j: (l, 0, j)),
        ],
        out_specs=pl.BlockSpec((1, 8, bn), lambda l, j: (l, 0, j)),
        compiler_params=_cp("arbitrary", "arbitrary"),
        name="adaln",
    )(cond8, w_ada, b_ada.reshape(depth, 1, n))


def _in_proj_kernel(x_ref, nw_ref, sh_ref, sc_ref, wa_ref, wb_ref, wz_ref, o_ref, z_ref, h_scr, *, na):
    j = pl.program_id(1)

    @pl.when(j == 0)
    def _():
        h = (_rms(x_ref[...]) * nw_ref[...]) * (1.0 + sc_ref[...]) + sh_ref[...]
        hb = h.astype(BF16)
        h_scr[...] = hb
        z_ref[...] = _dot(hb, wz_ref[...])

    @pl.when(j < na)
    def _():
        o_ref[...] = _dot(h_scr[...], wa_ref[...].astype(BF16)).astype(o_ref.dtype)

    @pl.when(j >= na)
    def _():
        o_ref[...] = _dot(h_scr[...], wb_ref[...]).astype(o_ref.dtype)


def _in_proj(x, norm_w, mods, w_in, layer, n_a, w_b, w_z, S):
    t, d = x.shape
    bm, bn = min(1024, S), 1024
    na, nb = n_a // bn, w_b.shape[1] // bn
    seg = lambda i: (i * bm) // S
    return pl.pallas_call(
        functools.partial(_in_proj_kernel, na=na),
        out_shape=(jax.ShapeDtypeStruct((t, (na + nb) * bn), BF16), jax.ShapeDtypeStruct((t, LANES), F32)),
        grid=(t // bm, na + nb),
        in_specs=[
            pl.BlockSpec((bm, d), lambda i, j: (i, 0), pipeline_mode=pl.Buffered(1)),
            pl.BlockSpec((1, d), lambda i, j: (0, 0)),
            pl.BlockSpec((None, 1, d), lambda i, j: (seg(i) * N_MOD + 0, 0, 0)),
            pl.BlockSpec((None, 1, d), lambda i, j: (seg(i) * N_MOD + 1, 0, 0)),
            pl.BlockSpec((None, d, bn), lambda i, j: (layer, 0, jnp.minimum(j, na - 1))),
            pl.BlockSpec((d, bn), lambda i, j: (0, jnp.maximum(j - na, 0))),
            pl.BlockSpec((d, LANES), lambda i, j: (0, 0)),
        ],
        out_specs=(pl.BlockSpec((bm, bn), lambda i, j: (i, j)), pl.BlockSpec((bm, LANES), lambda i, j: (i, 0))),
        scratch_shapes=[pltpu.VMEM((bm, d), BF16)],
        compiler_params=_cp("arbitrary", "arbitrary"),
        name="in_proj",
    )(x, norm_w, mods, mods, w_in, w_b, w_z)


CONV_TR, CONV_TC = 64, 256


def _conv_kernel(uv, ug, uvp, ugp, uvn, ugn, cw_ref, cb_ref, cn_ref, o_ref, buf, cv, *, tb, S, Lp, Ls, K):
    row0 = pl.program_id(0) * tb
    pos = jnp.where(row0 < S, lax.rem(row0, Lp), lax.rem(row0 - S, Ls))
    seq_len = jnp.where(row0 < S, Lp, Ls)
    at_start = pos == 0
    at_end = pos + tb == seq_len
    c = uv.shape[1]
    pad = K // 2

    def glu(a, b):
        return a[...].astype(F32) * _sigmoid(b[...].astype(F32))

    rows = tb + 2 * CONV_HALO
    buf[0, 0:CONV_HALO, :] = jnp.where(at_start, 0.0, glu(uvp, ugp))
    buf[0, CONV_HALO:CONV_HALO + tb, :] = glu(uv, ug)
    buf[0, CONV_HALO + tb:, :] = jnp.where(at_end, 0.0, glu(uvn, ugn))
    for p in range(1, SUBLANES):
        buf[p, 0:rows - SUBLANES, :] = buf[0, p:p + rows - SUBLANES, :]

    for r0 in range(0, tb, CONV_TR):
        for c0 in range(0, c, CONV_TC):
            acc = jnp.broadcast_to(cb_ref[:, c0:c0 + CONV_TC], (CONV_TR, CONV_TC))
            for k in range(K):
                start = CONV_HALO - pad + k
                p = start % SUBLANES
                a = start - p + r0
                acc = acc + cw_ref[k:k + 1, c0:c0 + CONV_TC] * buf[p, a:a + CONV_TR, c0:c0 + CONV_TC]
            cv[r0:r0 + CONV_TR, c0:c0 + CONV_TC] = acc

    y = _rms(cv[...]) * cn_ref[...]
    o_ref[...] = _silu(y).astype(o_ref.dtype)


def _conv_branch(proj, conv_w, conv_b, conv_norm, S, Lp, Ls):
    t = proj.shape[0]
    k, c = conv_w.shape
    tb = 256
    nh = tb // CONV_HALO
    last = t // CONV_HALO - 1
    main = lambda col: pl.BlockSpec((tb, c), lambda i: (i, col))
    prev = lambda col: pl.BlockSpec((CONV_HALO, c), lambda i: (jnp.maximum(i * nh - 1, 0), col))
    nxt = lambda col: pl.BlockSpec((CONV_HALO, c), lambda i: (jnp.minimum((i + 1) * nh, last), col))
    return pl.pallas_call(
        functools.partial(_conv_kernel, tb=tb, S=S, Lp=Lp, Ls=Ls, K=k),
        out_shape=jax.ShapeDtypeStruct((t, c), BF16),
        grid=(t // tb,),
        in_specs=[main(0), main(1), prev(0), prev(1), nxt(0), nxt(1),
                  pl.BlockSpec((k, c), lambda i: (0, 0)),
                  pl.BlockSpec((1, c), lambda i: (0, 0)),
                  pl.BlockSpec((1, c), lambda i: (0, 0))],
        out_specs=pl.BlockSpec((tb, c), lambda i: (i, 0)),
        scratch_shapes=[pltpu.VMEM((SUBLANES, tb + 2 * CONV_HALO, c), F32), pltpu.VMEM((tb, c), F32)],
        compiler_params=_cp("arbitrary"),
        name="conv_branch",
    )(proj, proj, proj, proj, proj, proj, conv_w, conv_b.reshape(1, c), conv_norm.reshape(1, c))


def _log_sigmoid(x):
    return jnp.minimum(x, 0.0) - jnp.log(1.0 + jnp.exp(-jnp.abs(x)))


def _gla_masks():
    C, n = GLA_CHUNK, GLA_STEP_CHUNKS
    R = C * n
    r = np.arange(R)[:, None]
    s = np.arange(R)[None, :]
    tris, codes = [], []
    for reverse in (False, True):
        sr, sc = r // C, s // C
        before_eq = s <= r
        if reverse:
            sr, sc, before_eq = n - 1 - sr, n - 1 - sc, s >= r
        sr, sc = np.broadcast_to(sr, (R, R)), np.broadcast_to(sc, (R, R))
        code = np.full((R, R), 3, np.int32)
        code[(sr >= 2) & (sc <= 1)] = 2
        code[(sr == sc + 1) & (sr % 2 == 1)] = 1
        code[(sr == sc) & before_eq] = 0
        codes.append(code)
        tris.append((code == 0).astype(np.float32))
    return jnp.asarray(np.stack(tris), BF16), jnp.asarray(np.stack(codes), I32)


def _gla_dir(q_ref, k_ref, v_ref, z_ref, w2h, w2l, b2, tri, code, st_ref, reverse, scale):
    C, n = GLA_CHUNK, GLA_STEP_CHUNKS
    R = C * n
    zh, zl = _split2(z_ref[...])
    pre = _dot(zl, w2h) + _dot(zh, w2l) + _dot(zh, w2h) + b2
    la = _log_sigmoid(pre) / GLA_TAU
    lh, lm, ll = _split3(la)
    a1 = _dot(tri, ll) + _dot(tri, lm) + _dot(tri, lh)

    last = 0 if reverse else C - 1
    order = list(range(n))[::-1] if reverse else list(range(n))
    cend = [a1[c * C + last:c * C + last + 1, :] for c in range(n)]
    c0, c1, c2, c3 = (cend[c] for c in order)
    e_before = {order[1]: c0, order[2]: c0 + c1, order[3]: c0 + c1 + c2}
    e_after = {order[0]: c1 + c2 + c3, order[1]: c2 + c3, order[2]: c3}
    etot = c0 + c1 + c2 + c3

    q_in, k_in, k_end, q_st, k_st, q_mid, k_mid = ([] for _ in range(7))
    for c in range(n):
        rows = slice(c * C, (c + 1) * C)
        a = a1[rows, :]
        qi = (q_ref[rows, :].astype(F32) * scale) * jnp.exp(a)
        k32 = k_ref[rows, :].astype(F32)
        ke = k32 * jnp.exp(cend[c] - a)
        q_in.append(qi)
        k_in.append(k32 * jnp.exp(-a))
        k_end.append(ke)
        q_st.append(qi * jnp.exp(e_before[c]) if c in e_before else qi)
        k_st.append(ke * jnp.exp(e_after[c]) if c in e_after else ke)
        q_mid.append(qi * jnp.exp(c2) if c == order[3] else qi)
        k_mid.append(ke * jnp.exp(c1) if c == order[0] else ke)
    cat = lambda xs: jnp.concatenate([x.astype(BF16) for x in xs], axis=0)
    q_in, q_st, k_st, q_mid, k_mid = cat(q_in), cat(q_st), cat(k_st), cat(q_mid), cat(k_mid)
    k_both = cat(k_in + k_end)

    s01 = _dot_nt(q_in, k_both)
    s2 = _dot_nt(q_mid, k_mid)
    p = jnp.where(code == 0, s01[:, :R], jnp.where(code == 1, s01[:, R:], jnp.where(code == 2, s2, 0.0)))
    st = st_ref[...]
    v = v_ref[...]
    o = _dot(p.astype(BF16), v) + _dot_nt(q_st, st.astype(BF16))
    st_ref[...] = st * jnp.exp(etot) + _dot_tn(v, k_st)
    return o


def _gla_kernel(*refs, nsteps, scale, has_init, n_aliased, has_final):
    qf, kf, vf, zf, qb, kb, vb, zb, w2h_ref, w2l_ref, b2_ref, gn_ref, g_ref, tri_ref, code_ref = refs[:15]
    rest = list(refs[15:])
    s0_ref = rest.pop(0) if has_init else None
    rest = rest[n_aliased:]
    o_ref = rest.pop(0)
    sfin_ref = rest.pop(0) if has_final else None
    o_acc, stf, stb = rest

    c = pl.program_id(2)
    R = GLA_CHUNK * GLA_STEP_CHUNKS

    @pl.when(c == 0)
    def _():
        o_acc[...] = jnp.zeros_like(o_acc)
        if has_init:
            stf[...] = s0_ref[0].T
            stb[...] = s0_ref[1].T
        else:
            stf[...] = jnp.zeros_like(stf)
            stb[...] = jnp.zeros_like(stb)

    of = _gla_dir(qf, kf, vf, zf, w2h_ref[0], w2l_ref[0], b2_ref[0], tri_ref[0], code_ref[0], stf, False, scale)
    ob = _gla_dir(qb, kb, vb, zb, w2h_ref[1], w2l_ref[1], b2_ref[1], tri_ref[1], code_ref[1], stb, True, scale)
    o_acc[pl.ds(pl.multiple_of(c * R, R), R), :] += of
    o_acc[pl.ds(pl.multiple_of((nsteps - 1 - c) * R, R), R), :] += ob

    @pl.when(c == nsteps - 1)
    def _():
        o = _rms(o_acc[...]) * gn_ref[...]
        o_ref[...] = (o * _silu(g_ref[...].astype(F32))).astype(o_ref.dtype)
        if has_final:
            sfin_ref[0] = stf[...].T
            sfin_ref[1] = stb[...].T


def _gla(proj, z, w2h, w2l, b2, gla_norm, tri, code, *, row_off, n_seq, L, H, HK, HV, col_q, col_k, col_v, col_g,
         s0=None, s0_layer=0, o_prev=None, ns_prev=None, ns_shape=None, ns_layer=0):
    t = proj.shape[0]
    R = GLA_CHUNK * GLA_STEP_CHUNKS
    nsteps = L // R
    rb0 = row_off // R
    has_init = s0 is not None
    has_final = ns_shape is not None
    fwd = lambda b, h, c: rb0 + b * nsteps + c
    bwd = lambda b, h, c: rb0 + b * nsteps + nsteps - 1 - c
    whole = lambda shape: pl.BlockSpec(shape, lambda b, h, c: (0,) * len(shape))

    def specs(rowmap):
        return [pl.BlockSpec((R, HK), lambda b, h, c: (rowmap(b, h, c), col_q + h)),
                pl.BlockSpec((R, HK), lambda b, h, c: (rowmap(b, h, c), col_k + h)),
                pl.BlockSpec((R, HV), lambda b, h, c: (rowmap(b, h, c), col_v + h)),
                pl.BlockSpec((R, LANES), lambda b, h, c: (rowmap(b, h, c), 0))]

    in_specs = specs(fwd) + specs(bwd) + [
        pl.BlockSpec((2, LANES, HK), lambda b, h, c: (0, 0, h)),
        pl.BlockSpec((2, LANES, HK), lambda b, h, c: (0, 0, h)),
        pl.BlockSpec((2, 1, HK), lambda b, h, c: (0, 0, h)),
        whole((1, HV)),
        pl.BlockSpec((L, HV), lambda b, h, c: (row_off // L + b, col_g + h)),
        whole((2, R, R)),
        whole((2, R, R)),
    ]
    args = [proj, proj, proj, z, proj, proj, proj, z, w2h, w2l, b2, gla_norm.reshape(1, HV), proj, tri, code]
    if has_init:
        in_specs.append(pl.BlockSpec((None, None, 2, None, HK, HV), lambda b, h, c: (b, s0_layer, 0, h, 0, 0)))
        args.append(s0)
    aliases = {}
    out_shape = [jax.ShapeDtypeStruct((t, H * HV), BF16)]
    out_specs = [pl.BlockSpec((L, HV), lambda b, h, c: (row_off // L + b, h))]
    if o_prev is not None:
        aliases[len(args)] = 0
        in_specs.append(pl.BlockSpec(memory_space=pl.ANY))
        args.append(o_prev)
    if has_final:
        out_shape.append(jax.ShapeDtypeStruct(ns_shape, F32))
        out_specs.append(pl.BlockSpec((None, None, 2, None, HK, HV), lambda b, h, c: (b, ns_layer, 0, h, 0, 0)))
        if ns_prev is not None:
            aliases[len(args)] = 1
            in_specs.append(pl.BlockSpec(memory_space=pl.ANY))
            args.append(ns_prev)
    return pl.pallas_call(
        functools.partial(_gla_kernel, nsteps=nsteps, scale=float(HK) ** -0.5,
                          has_init=has_init, n_aliased=len(aliases), has_final=has_final),
        out_shape=tuple(out_shape),
        grid=(n_seq, H, nsteps),
        in_specs=in_specs,
        out_specs=tuple(out_specs),
        scratch_shapes=[pltpu.VMEM((L, HV), F32), pltpu.VMEM((HV, HK), F32), pltpu.VMEM((HV, HK), F32)],
        input_output_aliases=aliases,
        compiler_params=_cp("arbitrary", "arbitrary", "arbitrary"),
        name="gla_init" if has_init else "gla_zero",
    )(*args)


def _branch_kernel(u_ref, og_ref, ga_ref, gb_ref, wc_ref, wg_ref, m_ref):
    a = _dot(u_ref[...], wc_ref[...])
    b = _dot(og_ref[...], wg_ref[...])
    m = _sigmoid(ga_ref[...].astype(F32)) * a + _sigmoid(gb_ref[...].astype(F32)) * b
    m_ref[...] = m.astype(m_ref.dtype)


def _branches(u_act, og, proj, w_conv_out, w_gla_out, col_ga, col_gb):
    t, c = u_act.shape
    d = w_conv_out.shape[1]
    dv = w_gla_out.shape[0]
    bm = 512
    return pl.pallas_call(
        _branch_kernel,
        out_shape=jax.ShapeDtypeStruct((t, d), BF16),
        grid=(t // bm,),
        in_specs=[
            pl.BlockSpec((bm, c), lambda i: (i, 0)),
            pl.BlockSpec((bm, dv), lambda i: (i, 0)),
            pl.BlockSpec((bm, d), lambda i: (i, col_ga)),
            pl.BlockSpec((bm, d), lambda i: (i, col_gb)),
            pl.BlockSpec((c, d), lambda i: (0, 0)),
            pl.BlockSpec((dv, d), lambda i: (0, 0)),
        ],
        out_specs=pl.BlockSpec((bm, d), lambda i: (i, 0)),
        compiler_params=_cp("arbitrary"),
        name="branches",
    )(u_act, og, proj, proj, w_conv_out, w_gla_out)


def _mix_out_kernel(m_ref, w_ref, x_ref, g1_ref, nw_ref, sh_ref, sc_ref, wrh_ref, wrl_ref, xo_ref, h_ref, lg_ref):
    x = x_ref[...] + g1_ref[...] * _dot(m_ref[...], w_ref[...])
    xo_ref[...] = x
    h = (_rms(x) * nw_ref[...]) * (1.0 + sc_ref[...]) + sh_ref[...]
    hh, hl = _split2(h)
    h_ref[...] = hh
    lg_ref[...] = _dot_nt(wrl_ref[...], hh) + _dot_nt(wrh_ref[...], hl) + _dot_nt(wrh_ref[...], hh)


def _mix_out(merged, w_mix, x, mods, norm_w, wr_hi, wr_lo, S):
    t, d = x.shape
    e = wr_hi.shape[0]
    bm = 512
    seg = lambda i: (i * bm) // S
    mod = lambda which: pl.BlockSpec((None, 1, d), lambda i: (seg(i) * N_MOD + which, 0, 0))
    return pl.pallas_call(
        _mix_out_kernel,
        out_shape=(jax.ShapeDtypeStruct((t, d), F32), jax.ShapeDtypeStruct((t, d), BF16),
                   jax.ShapeDtypeStruct((e, t), F32)),
        grid=(t // bm,),
        in_specs=[
            pl.BlockSpec((bm, d), lambda i: (i, 0)),
            pl.BlockSpec((d, d), lambda i: (0, 0)),
            pl.BlockSpec((bm, d), lambda i: (i, 0)),
            mod(2),
            pl.BlockSpec((1, d), lambda i: (0, 0)),
            mod(3), mod(4),
            pl.BlockSpec((e, d), lambda i: (0, 0)),
            pl.BlockSpec((e, d), lambda i: (0, 0)),
        ],
        out_specs=(pl.BlockSpec((bm, d), lambda i: (i, 0)), pl.BlockSpec((bm, d), lambda i: (i, 0)),
                   pl.BlockSpec((e, bm), lambda i: (0, i))),
        compiler_params=_cp("arbitrary"),
        name="mix_out",
    )(merged, w_mix, x, mods, norm_w, mods, mods, wr_hi, wr_lo)


def _route_kernel(lg_ref, slot_ref, gate_ref, cnt_ref, *, n_sets, L, cap):
    lg = lg_ref[...]
    e = lg.shape[0]
    ex = jnp.exp(lg - jnp.max(lg, axis=0, keepdims=True))
    aff = ex / jnp.sum(ex, axis=0, keepdims=True)
    bits = pltpu.bitcast(aff, I32)
    sets = [bits[:, s * L:(s + 1) * L] for s in range(n_sets)]

    def body(it, vs):
        bit = jnp.left_shift(jnp.int32(1), 30 - it)
        out = []
        for a, v in zip(sets, vs):
            cand = v | bit
            cnt = jnp.sum(jnp.where(a >= cand, 1.0, 0.0), axis=1, keepdims=True)
            out.append(jnp.where(cnt >= cap, cand, v))
        return tuple(out)

    thr = lax.fori_loop(0, 31, body, tuple(jnp.zeros((e, 1), I32) for _ in range(n_sets)))

    pb = ROUTE_BLOCK
    r = lax.broadcasted_iota(I32, (pb, pb), 0)
    c = lax.broadcasted_iota(I32, (pb, pb), 1)
    tri = jnp.where(r <= c, 1.0, 0.0).astype(BF16)
    cnt_ref[...] = jnp.full(cnt_ref.shape, n_sets * cap, I32)
    for s in range(n_sets):
        a, v = sets[s], thr[s]
        n_gt = jnp.sum(jnp.where(a > v, 1.0, 0.0), axis=1, keepdims=True)
        need = cap - n_gt
        carry_eq = jnp.zeros((e, 1), F32)
        carry_sel = jnp.zeros((e, 1), F32)
        for b0 in range(0, L, pb):
            blk = (s * L + b0) // pb
            cnt_ref[:, blk:blk + 1] = carry_sel.astype(I32) + s * cap
            ab = a[:, b0:b0 + pb]
            eq = jnp.where(ab == v, 1.0, 0.0)
            rank_eq = _dot(eq.astype(BF16), tri) - eq + carry_eq
            sel = jnp.where(ab > v, 1.0, jnp.where(rank_eq < need, eq, 0.0))
            rank_sel = _dot(sel.astype(BF16), tri) - sel + carry_sel
            carry_eq = carry_eq + jnp.sum(eq, axis=1, keepdims=True)
            carry_sel = carry_sel + jnp.sum(sel, axis=1, keepdims=True)
            cols = slice(s * L + b0, s * L + b0 + pb)
            slot_ref[:, cols] = jnp.where(sel > 0.0, rank_sel.astype(I32) + s * cap, -1)
            gate_ref[:, cols] = jnp.where(sel > 0.0, aff[:, cols], 0.0)


def _route(logits_t, *, col0, n_groups, n_sets, L, cap):
    e = logits_t.shape[0]
    n = n_sets * L
    assert n // ROUTE_BLOCK < LANES
    return pl.pallas_call(
        functools.partial(_route_kernel, n_sets=n_sets, L=L, cap=cap),
        out_shape=(jax.ShapeDtypeStruct((e, n_groups * n), I32), jax.ShapeDtypeStruct((e, n_groups * n), F32),
                   jax.ShapeDtypeStruct((e, n_groups * LANES), I32)),
        grid=(n_groups,),
        in_specs=[pl.BlockSpec((e, n), lambda g: (0, col0 + g))],
        out_specs=(pl.BlockSpec((e, n), lambda g: (0, g)), pl.BlockSpec((e, n), lambda g: (0, g)),
                   pl.BlockSpec((e, LANES), lambda g: (0, g))),
        compiler_params=_cp("arbitrary"),
        name="route",
    )(logits_t)


GATHER_TOKENS = 512
GATHER_SLOTS = 128
SCATTER_TOKENS = 512
SCATTER_WINDOW = 128
BF16_ROWS = 16


def _gather_kernel(cnt_ref, h_ref, slot_ref, gate_ref, g_ref, gs_ref, acc, gs_acc, *, nslot, n_exp, n_cnt):
    s_tok = h_ref.shape[0]
    tk = min(GATHER_TOKENS, s_tok)
    sb = min(GATHER_SLOTS, nslot)
    base = (pl.program_id(0) * n_exp + pl.program_id(1)) * n_cnt
    acc[...] = jnp.zeros_like(acc)
    gs_acc[...] = jnp.zeros_like(gs_acc)
    for c in range(s_tok // tk):
        lo = cnt_ref[base + c * (tk // ROUTE_BLOCK)]
        hi = cnt_ref[base + (c + 1) * (tk // ROUTE_BLOCK)]

        def body(j, carry, c=c):
            off = pl.multiple_of(j * sb, sb)
            hit = slot_ref[:, c * tk:(c + 1) * tk] == lax.broadcasted_iota(I32, (sb, tk), 0) + off
            onehot = jnp.where(hit, 1.0, 0.0).astype(BF16)
            acc[pl.ds(off, sb), :] += _dot(onehot, h_ref[c * tk:(c + 1) * tk, :])
            gs_acc[pl.ds(off, sb), :] += jnp.sum(jnp.where(hit, gate_ref[:, c * tk:(c + 1) * tk], 0.0),
                                                 axis=1, keepdims=True)
            return carry

        lax.fori_loop(lo // sb, (hi + sb - 1) // sb, body, 0)
    g_ref[...] = acc[...].astype(g_ref.dtype)
    gs_ref[...] = gs_acc[...]


def _gather(cnt, h2, slot3, gate3, S, nslot, n_cnt):
    t, d = h2.shape
    e = slot3.shape[0]
    nseg = t // S
    return pl.pallas_call(
        functools.partial(_gather_kernel, nslot=nslot, n_exp=e, n_cnt=n_cnt),
        out_shape=(jax.ShapeDtypeStruct((e, nseg * nslot, d), BF16), jax.ShapeDtypeStruct((e, nseg * nslot, 1), F32)),
        grid_spec=pltpu.PrefetchScalarGridSpec(
            num_scalar_prefetch=1,
            grid=(nseg, e),
            in_specs=[
                pl.BlockSpec((S, d), lambda s, x, cnt: (s, 0)),
                pl.BlockSpec((None, 1, S), lambda s, x, cnt: (x, 0, s)),
                pl.BlockSpec((None, 1, S), lambda s, x, cnt: (x, 0, s)),
            ],
            out_specs=(pl.BlockSpec((None, nslot, d), lambda s, x, cnt: (x, s, 0)),
                       pl.BlockSpec((None, nslot, 1), lambda s, x, cnt: (x, s, 0))),
            scratch_shapes=[pltpu.VMEM((nslot, d), F32), pltpu.VMEM((nslot, 1), F32)],
        ),
        compiler_params=_cp("arbitrary", "arbitrary"),
        name="moe_gather",
    )(cnt, h2, slot3, gate3)


def _ffn_up_kernel(g_ref, wg_ref, wu_ref, a_ref):
    g = g_ref[...]
    a = _dot(g, wg_ref[...].astype(BF16))
    b = _dot(g, wu_ref[...].astype(BF16))
    a_ref[...] = (_silu(a) * b).astype(a_ref.dtype)


def _ffn_up(gathered, w_gate, w_up, layer):
    e, m, d = gathered.shape
    f = w_gate.shape[3]
    bf = 512
    return pl.pallas_call(
        _ffn_up_kernel,
        out_shape=jax.ShapeDtypeStruct((e, m, f), BF16),
        grid=(e, f // bf),
        in_specs=[
            pl.BlockSpec((None, m, d), lambda x, j: (x, 0, 0)),
            pl.BlockSpec((None, None, d, bf), lambda x, j: (layer, x, 0, j)),
            pl.BlockSpec((None, None, d, bf), lambda x, j: (layer, x, 0, j)),
        ],
        out_specs=pl.BlockSpec((None, m, bf), lambda x, j: (x, 0, j)),
        compiler_params=_cp("arbitrary", "arbitrary"),
        name="ffn_up",
    )(gathered, w_gate, w_up)


def _ffn_down_kernel(a_ref, wd_ref, gs_ref, y_ref):
    y = _dot(a_ref[...], wd_ref[...].astype(BF16))
    y_ref[...] = (y * gs_ref[...]).astype(y_ref.dtype)


def _ffn_down(act, w_down, gslot, layer):
    e, m, f = act.shape
    d = w_down.shape[3]
    bn = 512
    return pl.pallas_call(
        _ffn_down_kernel,
        out_shape=jax.ShapeDtypeStruct((e, m, d), BF16),
        grid=(e, d // bn),
        in_specs=[
            pl.BlockSpec((None, m, f), lambda x, j: (x, 0, 0)),
            pl.BlockSpec((None, None, f, bn), lambda x, j: (layer, x, 0, j)),
            pl.BlockSpec((None, m, 1), lambda x, j: (x, 0, 0)),
        ],
        out_specs=pl.BlockSpec((None, m, bn), lambda x, j: (x, 0, j)),
        compiler_params=_cp("arbitrary", "arbitrary"),
        name="ffn_down",
    )(act, w_down, gslot)


def _scatter_kernel(cnt_ref, slot_ref, x_ref, g2_ref, y_hbm, o_ref, ycat, ybuf, sems, sem_extra,
                    *, nslot, n_exp, n_cnt, S):
    i = pl.program_id(0)
    tb = o_ref.shape[0]
    w = min(SCATTER_WINDOW, nslot)
    seg = (i * tb) // S
    blk = (i * tb - seg * S) // ROUTE_BLOCK
    row0 = seg * nslot
    half = n_exp // 2

    def window(e, start, dst, sem):
        return pltpu.make_async_copy(y_hbm.at[e, pl.ds(row0 + start, w), :], dst, sem)

    starts, his = [], []
    for e in range(n_exp):
        base = (seg * n_exp + e) * n_cnt + blk
        lo = cnt_ref[base]
        his.append(cnt_ref[base + tb // ROUTE_BLOCK])
        starts.append(pl.multiple_of(jnp.minimum((lo // BF16_ROWS) * BF16_ROWS, nslot - w), BF16_ROWS))
        window(e, starts[e], ycat.at[pl.ds(e * w, w), :], sems.at[e]).start()

    def onehot(e, start, lower=None):
        ids = lax.broadcasted_iota(I32, (w, tb), 0) + start
        oh = jnp.where(slot_ref[e] == ids, 1.0, 0.0)
        if lower is not None:
            oh = jnp.where(ids >= lower, oh, 0.0)
        return oh.astype(BF16)

    for g in range(2):
        oh = jnp.concatenate([onehot(e, starts[e]) for e in range(g * half, (g + 1) * half)], axis=0)
        for e in range(g * half, (g + 1) * half):
            window(e, starts[e], ycat.at[pl.ds(e * w, w), :], sems.at[e]).wait()
        part = _dot_tn(oh, ycat[g * half * w:(g + 1) * half * w, :])
        if g == 0:
            o_ref[...] = part
        else:
            o_ref[...] += part

    for e in range(n_exp):
        n_extra = (jnp.maximum(his[e] - starts[e], 1) - 1) // w

        def body(p, carry, e=e):
            lower = starts[e] + (p + 1) * w
            st = pl.multiple_of(jnp.minimum(lower, nslot - w), BF16_ROWS)
            cp = window(e, st, ybuf, sem_extra.at[0])
            cp.start()
            cp.wait()
            o_ref[...] += _dot_tn(onehot(e, st, lower), ybuf[...])
            return carry

        lax.fori_loop(0, n_extra, body, 0)

    o_ref[...] = x_ref[...] + g2_ref[...] * o_ref[...]


def _scatter(cnt, slot3, yg, x, mods, S, nslot, n_cnt):
    t, d = x.shape
    e = slot3.shape[0]
    tb = min(SCATTER_TOKENS, S)
    w = min(SCATTER_WINDOW, nslot)
    assert e % 2 == 0 and nslot % BF16_ROWS == 0
    seg = lambda i: (i * tb) // S
    return pl.pallas_call(
        functools.partial(_scatter_kernel, nslot=nslot, n_exp=e, n_cnt=n_cnt, S=S),
        out_shape=jax.ShapeDtypeStruct((t, d), F32),
        grid_spec=pltpu.PrefetchScalarGridSpec(
            num_scalar_prefetch=1,
            grid=(t // tb,),
            in_specs=[
                pl.BlockSpec((e, 1, tb), lambda i, cnt: (0, 0, i)),
                pl.BlockSpec((tb, d), lambda i, cnt: (i, 0)),
                pl.BlockSpec((None, 1, d), lambda i, cnt: (seg(i) * N_MOD + 5, 0, 0)),
                pl.BlockSpec(memory_space=pl.ANY),
            ],
            out_specs=pl.BlockSpec((tb, d), lambda i, cnt: (i, 0)),
            scratch_shapes=[pltpu.VMEM((e * w, d), BF16), pltpu.VMEM((w, d), BF16),
                            pltpu.SemaphoreType.DMA((e,)), pltpu.SemaphoreType.DMA((1,))],
        ),
        compiler_params=_cp("arbitrary"),
        name="moe_scatter",
    )(cnt, slot3, x, mods, yg)


def _final_norm_kernel(x_ref, w_ref, o_ref):
    o_ref[...] = _rms(x_ref[...]) * w_ref[...]


def _final_norm(x, w, row_off, rows):
    d = x.shape[1]
    bm = 512
    off = row_off // bm
    return pl.pallas_call(
        _final_norm_kernel,
        out_shape=jax.ShapeDtypeStruct((rows, d), F32),
        grid=(rows // bm,),
        in_specs=[pl.BlockSpec((bm, d), lambda i: (off + i, 0)), pl.BlockSpec((1, d), lambda i: (0, 0))],
        out_specs=pl.BlockSpec((bm, d), lambda i: (i, 0)),
        compiler_params=_cp("arbitrary"),
        name="final_norm",
    )(x, w)


def kernel(x_prompt, x_sample, state_gla, c, c_ctx, w_in, conv_w, conv_b, conv_norm, w_conv_out, gla_gate_w2, gla_gate_b, gla_norm, w_gla_out, w_mix_out, w_ada, b_ada, norm_mix, norm_ffn, w_router, w_exp_gate, w_exp_up, w_exp_down, norm_final):
    B, SEQ, D = x_prompt.shape
    DB, DSEQ, _ = x_sample.shape
    DEPTH = w_in.shape[0]
    H, HK, HV = state_gla.shape[3:]
    DC = conv_w.shape[2]
    DK, DV = H * HK, H * HV
    RANK = gla_gate_w2.shape[2]
    E = w_router.shape[2]
    S = DSEQ
    R = GLA_CHUNK * GLA_STEP_CHUNKS
    assert B * SEQ == S, "prompt tokens must fill exactly one segment"
    assert 2 * RANK <= LANES and conv_w.shape[1] // 2 <= CONV_HALO
    assert SEQ % R == 0 and DSEQ % R == 0 and SEQ % ROUTE_BLOCK == 0 and S % GATHER_TOKENS == 0
    nseg = 1 + DB
    assert nseg <= 8
    T = nseg * S
    cap_p = (EC_CAPACITY_FACTOR * SEQ) // E
    cap_s = (EC_CAPACITY_FACTOR * DSEQ) // E
    nslot = B * cap_p
    assert nslot == cap_s
    n_cnt = S // ROUTE_BLOCK + 1

    x = _embed(x_prompt.reshape(B * SEQ, D), x_sample.reshape(DB * DSEQ, D), _pos_table(DSEQ, D), S)

    cond8 = jnp.zeros((8, D), F32).at[0].set(c_ctx).at[1:1 + DB].set(c)
    mods_all = _adaln(cond8, w_ada, b_ada).reshape(DEPTH, 8 * N_MOD, 1, D)

    z0 = 2 * DC + 2 * DK + 2 * DV
    col_q, col_k = (2 * DC) // HK, (2 * DC + DK) // HK
    col_v = (2 * DC + 2 * DK) // HV
    col_g = (2 * DC + 2 * DK + DV) // HV
    col_ga, col_gb = z0 // D, z0 // D + 1
    tri, code = _gla_masks()

    new_state = None
    ns_shape = (B, DEPTH, 2, H, HK, HV)
    for l in range(DEPTH):
        mods = mods_all[l]
        w_b = w_in[l, :, z0 + 2 * RANK:].astype(BF16)
        w_z = jnp.pad(w_in[l, :, z0:z0 + 2 * RANK], ((0, 0), (0, LANES - 2 * RANK))).astype(BF16)
        proj, z = _in_proj(x, norm_mix[l].reshape(1, D), mods, w_in, l, z0, w_b, w_z, S)

        u_act = _conv_branch(proj, conv_w[l], conv_b[l], conv_norm[l], S, SEQ, DSEQ)

        w2pad = jnp.zeros((2, LANES, DK), F32)
        w2pad = w2pad.at[0, :RANK].set(gla_gate_w2[l, 0]).at[1, RANK:2 * RANK].set(gla_gate_w2[l, 1])
        w2h = w2pad.astype(BF16)
        w2l = (w2pad - w2h.astype(F32)).astype(BF16)
        b2 = gla_gate_b[l].reshape(2, 1, DK)
        common = dict(H=H, HK=HK, HV=HV, col_q=col_q, col_k=col_k, col_v=col_v, col_g=col_g)
        og, new_state = _gla(proj, z, w2h, w2l, b2, gla_norm[l], tri, code, row_off=0, n_seq=B, L=SEQ,
                             ns_prev=new_state, ns_shape=ns_shape, ns_layer=l, **common)
        (og,) = _gla(proj, z, w2h, w2l, b2, gla_norm[l], tri, code, row_off=S, n_seq=DB, L=DSEQ,
                     s0=state_gla, s0_layer=l, o_prev=og, **common)

        merged = _branches(u_act, og, proj, w_conv_out[l].astype(BF16), w_gla_out[l].astype(BF16), col_ga, col_gb)
        wr_t = w_router[l].T
        wr_hi = wr_t.astype(BF16)
        wr_lo = (wr_t - wr_hi.astype(F32)).astype(BF16)
        x, h2, logits_t = _mix_out(merged, w_mix_out[l].astype(BF16), x, mods, norm_ffn[l].reshape(1, D),
                                   wr_hi, wr_lo, S)

        slot_p, gate_p, cnt_p = _route(logits_t, col0=0, n_groups=1, n_sets=B, L=SEQ, cap=cap_p)
        slot_s, gate_s, cnt_s = _route(logits_t, col0=1, n_groups=DB, n_sets=1, L=DSEQ, cap=cap_s)
        slot3 = jnp.concatenate([slot_p, slot_s], axis=1).reshape(E, 1, T)
        gate3 = jnp.concatenate([gate_p, gate_s], axis=1).reshape(E, 1, T)
        cnt = jnp.concatenate([cnt_p, cnt_s], axis=1).reshape(E, nseg, LANES)[:, :, :n_cnt]
        cnt = cnt.transpose(1, 0, 2).reshape(nseg * E * n_cnt)

        gathered, gslot = _gather(cnt, h2, slot3, gate3, S, nslot, n_cnt)
        act = _ffn_up(gathered, w_exp_gate, w_exp_up, l)
        yg = _ffn_down(act, w_exp_down, gslot, l)
        x = _scatter(cnt, slot3, yg, x, mods, S, nslot, n_cnt)

    nf = norm_final.reshape(1, D)
    y_prompt = _final_norm(x, nf, 0, B * SEQ).reshape(B, SEQ, D)
    y_sample = _final_norm(x, nf, S, DB * DSEQ).reshape(DB, DSEQ, D)
    return (y_prompt, y_sample, new_state)
```

```python
import functools

import jax
import jax.numpy as jnp
import numpy as np
from jax import lax
from jax.experimental import pallas as pl
from jax.experimental.pallas import tpu as pltpu

F32 = jnp.float32
BF16 = jnp.bfloat16
I32 = jnp.int32
HI = lax.Precision.HIGHEST

EPS = 1e-6
GRID_W = 64
POS_THETA = 10000.0
GLA_CHUNK = 64
GLA_STEP_CHUNKS = 4
GLA_TAU = 16.0
EC_CAPACITY_FACTOR = 2
N_MOD = 6
CONV_HALO = 16
SUBLANES = 8
LANES = 128
ROUTE_BLOCK = 256
VMEM_LIMIT = 56 * 1024 * 1024

NT_DIMS = (((1,), (1,)), ((), ()))
TN_DIMS = (((0,), (0,)), ((), ()))


def _cp(*sem):
    return pltpu.CompilerParams(dimension_semantics=sem, vmem_limit_bytes=VMEM_LIMIT)


def _sigmoid(x):
    return 1.0 / (1.0 + jnp.exp(-x))


def _silu(x):
    return x * _sigmoid(x)


def _rms(x):
    return x * lax.rsqrt(jnp.mean(x * x, axis=-1, keepdims=True) + EPS)


def _dot(a, b):
    return jnp.dot(a, b, preferred_element_type=F32)


def _dot_nt(a, b):
    return lax.dot_general(a, b, NT_DIMS, preferred_element_type=F32)


def _dot_tn(a, b):
    return lax.dot_general(a, b, TN_DIMS, preferred_element_type=F32)


def _split2(x):
    hi = x.astype(BF16)
    return hi, (x - hi.astype(F32)).astype(BF16)


def _split3(x):
    hi = x.astype(BF16)
    r = x - hi.astype(F32)
    mid = r.astype(BF16)
    return hi, mid, (r - mid.astype(F32)).astype(BF16)


def _embed_kernel(xp_ref, xs_ref, emb_ref, o_ref, *, nb_p):
    i = pl.program_id(0)

    @pl.when(i < nb_p)
    def _():
        o_ref[...] = xp_ref[...]

    @pl.when(i >= nb_p)
    def _():
        o_ref[...] = xs_ref[...] + emb_ref[...]


def _pos_table(n, d):
    rows = n // GRID_W
    rr, cc = jnp.meshgrid(jnp.arange(rows, dtype=F32), jnp.arange(GRID_W, dtype=F32), indexing="ij")
    rr = rr.reshape(-1, 1)
    cc = cc.reshape(-1, 1)
    quarter = d // 4
    freqs = POS_THETA ** (-jnp.arange(quarter, dtype=F32) / quarter)
    return jnp.concatenate([jnp.sin(rr * freqs), jnp.cos(rr * freqs), jnp.sin(cc * freqs), jnp.cos(cc * freqs)], axis=-1)


def _embed(xp2, xs2, emb, S):
    tp, d = xp2.shape
    ts = xs2.shape[0]
    bm = min(512, S)
    nb_p, nb_seq = tp // bm, S // bm
    return pl.pallas_call(
        functools.partial(_embed_kernel, nb_p=nb_p),
        out_shape=jax.ShapeDtypeStruct((tp + ts, d), F32),
        grid=((tp + ts) // bm,),
        in_specs=[
            pl.BlockSpec((bm, d), lambda i: (jnp.minimum(i, nb_p - 1), 0)),
            pl.BlockSpec((bm, d), lambda i: (jnp.maximum(i - nb_p, 0), 0)),
            pl.BlockSpec((bm, d), lambda i: (jnp.maximum(i - nb_p, 0) % nb_seq, 0)),
        ],
        out_specs=pl.BlockSpec((bm, d), lambda i: (i, 0)),
        compiler_params=_cp("arbitrary"),
        name="embed",
    )(xp2, xs2, emb)


def _ada_kernel(c_ref, w_ref, b_ref, o_ref):
    s = _silu(c_ref[...])
    o_ref[0] = jnp.dot(s, w_ref[0], precision=HI, preferred_element_type=F32) + b_ref[0]


def _adaln(cond8, w_ada, b_ada):
    depth, d, n = w_ada.shape
    bn = 1024
    return pl.pallas_call(
        _ada_kernel,
        out_shape=jax.ShapeDtypeStruct((depth, 8, n), F32),
        grid=(depth, n // bn),
        in_specs=[
            pl.BlockSpec((8, d), lambda l, j: (0, 0)),
            pl.BlockSpec((1, d, bn), lambda l, j: (l, 0, j)),
            pl.BlockSpec((1, 1, bn), lambda l, j: (l, 0, j)),
        ],
        out_specs=pl.BlockSpec((1, 8, bn), lambda l, j: (l, 0, j)),
        compiler_params=_cp("arbitrary", "arbitrary"),
        name="adaln",
    )(cond8, w_ada, b_ada.reshape(depth, 1, n))


def _in_proj_kernel(x_ref, nw_ref, sh_ref, sc_ref, wa_ref, wb_ref, wz_ref, o_ref, z_ref, h_scr, *, na):
    j = pl.program_id(1)

    @pl.when(j == 0)
    def _():
        h = (_rms(x_ref[...]) * nw_ref[...]) * (1.0 + sc_ref[...]) + sh_ref[...]
        hb = h.astype(BF16)
        h_scr[...] = hb
        z_ref[...] = _dot(hb, wz_ref[...])

    @pl.when(j < na)
    def _():
        o_ref[...] = _dot(h_scr[...], wa_ref[...].astype(BF16)).astype(o_ref.dtype)

    @pl.when(j >= na)
    def _():
        o_ref[...] = _dot(h_scr[...], wb_ref[...]).astype(o_ref.dtype)


def _in_proj(x, norm_w, mods, w_in, layer, n_a, w_b, w_z, S):
    t, d = x.shape
    bm, bn = min(1024, S), 1024
    na, nb = n_a // bn, w_b.shape[1] // bn
    seg = lambda i: (i * bm) // S
    return pl.pallas_call(
        functools.partial(_in_proj_kernel, na=na),
        out_shape=(jax.ShapeDtypeStruct((t, (na + nb) * bn), BF16), jax.ShapeDtypeStruct((t, LANES), F32)),
        grid=(t // bm, na + nb),
        in_specs=[
            pl.BlockSpec((bm, d), lambda i, j: (i, 0), pipeline_mode=pl.Buffered(1)),
            pl.BlockSpec((1, d), lambda i, j: (0, 0)),
            pl.BlockSpec((None, 1, d), lambda i, j: (seg(i) * N_MOD + 0, 0, 0)),
            pl.BlockSpec((None, 1, d), lambda i, j: (seg(i) * N_MOD + 1, 0, 0)),
            pl.BlockSpec((None, d, bn), lambda i, j: (layer, 0, jnp.minimum(j, na - 1))),
            pl.BlockSpec((d, bn), lambda i, j: (0, jnp.maximum(j - na, 0))),
            pl.BlockSpec((d, LANES), lambda i, j: (0, 0)),
        ],
        out_specs=(pl.BlockSpec((bm, bn), lambda i, j: (i, j)), pl.BlockSpec((bm, LANES), lambda i, j: (i, 0))),
        scratch_shapes=[pltpu.VMEM((bm, d), BF16)],
        compiler_params=_cp("arbitrary", "arbitrary"),
        name="in_proj",
    )(x, norm_w, mods, mods, w_in, w_b, w_z)


CONV_TR, CONV_TC = 64, 256


def _conv_kernel(uv, ug, uvp, ugp, uvn, ugn, cw_ref, cb_ref, cn_ref, o_ref, buf, cv, *, tb, S, Lp, Ls, K):
    row0 = pl.program_id(0) * tb
    pos = jnp.where(row0 < S, lax.rem(row0, Lp), lax.rem(row0 - S, Ls))
    seq_len = jnp.where(row0 < S, Lp, Ls)
    at_start = pos == 0
    at_end = pos + tb == seq_len
    c = uv.shape[1]
    pad = K // 2

    def glu(a, b):
        return a[...].astype(F32) * _sigmoid(b[...].astype(F32))

    rows = tb + 2 * CONV_HALO
    buf[0, 0:CONV_HALO, :] = jnp.where(at_start, 0.0, glu(uvp, ugp))
    buf[0, CONV_HALO:CONV_HALO + tb, :] = glu(uv, ug)
    buf[0, CONV_HALO + tb:, :] = jnp.where(at_end, 0.0, glu(uvn, ugn))
    for p in range(1, SUBLANES):
        buf[p, 0:rows - SUBLANES, :] = buf[0, p:p + rows - SUBLANES, :]

    for r0 in range(0, tb, CONV_TR):
        for c0 in range(0, c, CONV_TC):
            acc = jnp.broadcast_to(cb_ref[:, c0:c0 + CONV_TC], (CONV_TR, CONV_TC))
            for k in range(K):
                start = CONV_HALO - pad + k
                p = start % SUBLANES
                a = start - p + r0
                acc = acc + cw_ref[k:k + 1, c0:c0 + CONV_TC] * buf[p, a:a + CONV_TR, c0:c0 + CONV_TC]
            cv[r0:r0 + CONV_TR, c0:c0 + CONV_TC] = acc

    y = _rms(cv[...]) * cn_ref[...]
    o_ref[...] = _silu(y).astype(o_ref.dtype)


def _conv_branch(proj, conv_w, conv_b, conv_norm, S, Lp, Ls):
    t = proj.shape[0]
    k, c = conv_w.shape
    tb = 256
    nh = tb // CONV_HALO
    last = t // CONV_HALO - 1
    main = lambda col: pl.BlockSpec((tb, c), lambda i: (i, col))
    prev = lambda col: pl.BlockSpec((CONV_HALO, c), lambda i: (jnp.maximum(i * nh - 1, 0), col))
    nxt = lambda col: pl.BlockSpec((CONV_HALO, c), lambda i: (jnp.minimum((i + 1) * nh, last), col))
    return pl.pallas_call(
        functools.partial(_conv_kernel, tb=tb, S=S, Lp=Lp, Ls=Ls, K=k),
        out_shape=jax.ShapeDtypeStruct((t, c), BF16),
        grid=(t // tb,),
        in_specs=[main(0), main(1), prev(0), prev(1), nxt(0), nxt(1),
                  pl.BlockSpec((k, c), lambda i: (0, 0)),
                  pl.BlockSpec((1, c), lambda i: (0, 0)),
                  pl.BlockSpec((1, c), lambda i: (0, 0))],
        out_specs=pl.BlockSpec((tb, c), lambda i: (i, 0)),
        scratch_shapes=[pltpu.VMEM((SUBLANES, tb + 2 * CONV_HALO, c), F32), pltpu.VMEM((tb, c), F32)],
        compiler_params=_cp("arbitrary"),
        name="conv_branch",
    )(proj, proj, proj, proj, proj, proj, conv_w, conv_b.reshape(1, c), conv_norm.reshape(1, c))


def _log_sigmoid(x):
    return jnp.minimum(x, 0.0) - jnp.log(1.0 + jnp.exp(-jnp.abs(x)))


def _gla_masks():
    C, n = GLA_CHUNK, GLA_STEP_CHUNKS
    R = C * n
    r = np.arange(R)[:, None]
    s = np.arange(R)[None, :]
    tris, codes = [], []
    for reverse in (False, True):
        sr, sc = r // C, s // C
        before_eq = s <= r
        if reverse:
            sr, sc, before_eq = n - 1 - sr, n - 1 - sc, s >= r
        sr, sc = np.broadcast_to(sr, (R, R)), np.broadcast_to(sc, (R, R))
        code = np.full((R, R), 3, np.int32)
        code[(sr >= 2) & (sc <= 1)] = 2
        code[(sr == sc + 1) & (sr % 2 == 1)] = 1
        code[(sr == sc) & before_eq] = 0
        codes.append(code)
        tris.append((code == 0).astype(np.float32))
    return jnp.asarray(np.stack(tris), BF16), jnp.asarray(np.stack(codes), I32)


def _gla_dir(q_ref, k_ref, v_ref, z_ref, w2h, w2l, b2, tri, code, st_ref, reverse, scale):
    C, n = GLA_CHUNK, GLA_STEP_CHUNKS
    R = C * n
    zh, zl = _split2(z_ref[...])
    pre = _dot(zl, w2h) + _dot(zh, w2l) + _dot(zh, w2h) + b2
    la = _log_sigmoid(pre) / GLA_TAU
    lh, lm, ll = _split3(la)
    a1 = _dot(tri, ll) + _dot(tri, lm) + _dot(tri, lh)

    last = 0 if reverse else C - 1
    order = list(range(n))[::-1] if reverse else list(range(n))
    cend = [a1[c * C + last:c * C + last + 1, :] for c in range(n)]
    c0, c1, c2, c3 = (cend[c] for c in order)
    e_before = {order[1]: c0, order[2]: c0 + c1, order[3]: c0 + c1 + c2}
    e_after = {order[0]: c1 + c2 + c3, order[1]: c2 + c3, order[2]: c3}
    etot = c0 + c1 + c2 + c3

    q_in, k_in, k_end, q_st, k_st, q_mid, k_mid = ([] for _ in range(7))
    for c in range(n):
        rows = slice(c * C, (c + 1) * C)
        a = a1[rows, :]
        qi = (q_ref[rows, :].astype(F32) * scale) * jnp.exp(a)
        k32 = k_ref[rows, :].astype(F32)
        ke = k32 * jnp.exp(cend[c] - a)
        q_in.append(qi)
        k_in.append(k32 * jnp.exp(-a))
        k_end.append(ke)
        q_st.append(qi * jnp.exp(e_before[c]) if c in e_before else qi)
        k_st.append(ke * jnp.exp(e_after[c]) if c in e_after else ke)
        q_mid.append(qi * jnp.exp(c2) if c == order[3] else qi)
        k_mid.append(ke * jnp.exp(c1) if c == order[0] else ke)
    cat = lambda xs: jnp.concatenate([x.astype(BF16) for x in xs], axis=0)
    q_in, q_st, k_st, q_mid, k_mid = cat(q_in), cat(q_st), cat(k_st), cat(q_mid), cat(k_mid)
    k_both = cat(k_in + k_end)

    s01 = _dot_nt(q_in, k_both)
    s2 = _dot_nt(q_mid, k_mid)
    p = jnp.where(code == 0, s01[:, :R], jnp.where(code == 1, s01[:, R:], jnp.where(code == 2, s2, 0.0)))
    st = st_ref[...]
    v = v_ref[...]
    o = _dot(p.astype(BF16), v) + _dot_nt(q_st, st.astype(BF16))
    st_ref[...] = st * jnp.exp(etot) + _dot_tn(v, k_st)
    return o


def _gla_kernel(*refs, nsteps, scale, has_init, n_aliased, has_final):
    qf, kf, vf, zf, qb, kb, vb, zb, w2h_ref, w2l_ref, b2_ref, gn_ref, g_ref, tri_ref, code_ref = refs[:15]
    rest = list(refs[15:])
    s0_ref = rest.pop(0) if has_init else None
    rest = rest[n_aliased:]
    o_ref = rest.pop(0)
    sfin_ref = rest.pop(0) if has_final else None
    o_acc, stf, stb = rest

    c = pl.program_id(2)
    R = GLA_CHUNK * GLA_STEP_CHUNKS

    @pl.when(c == 0)
    def _():
        o_acc[...] = jnp.zeros_like(o_acc)
        if has_init:
            stf[...] = s0_ref[0].T
            stb[...] = s0_ref[1].T
        else:
            stf[...] = jnp.zeros_like(stf)
            stb[...] = jnp.zeros_like(stb)

    of = _gla_dir(qf, kf, vf, zf, w2h_ref[0], w2l_ref[0], b2_ref[0], tri_ref[0], code_ref[0], stf, False, scale)
    ob = _gla_dir(qb, kb, vb, zb, w2h_ref[1], w2l_ref[1], b2_ref[1], tri_ref[1], code_ref[1], stb, True, scale)
    o_acc[pl.ds(pl.multiple_of(c * R, R), R), :] += of
    o_acc[pl.ds(pl.multiple_of((nsteps - 1 - c) * R, R), R), :] += ob

    @pl.when(c == nsteps - 1)
    def _():
        o = _rms(o_acc[...]) * gn_ref[...]
        o_ref[...] = (o * _silu(g_ref[...].astype(F32))).astype(o_ref.dtype)
        if has_final:
            sfin_ref[0] = stf[...].T
            sfin_ref[1] = stb[...].T


def _gla(proj, z, w2h, w2l, b2, gla_norm, tri, code, *, row_off, n_seq, L, H, HK, HV, col_q, col_k, col_v, col_g,
         s0=None, s0_layer=0, o_prev=None, ns_prev=None, ns_shape=None, ns_layer=0):
    t = proj.shape[0]
    R = GLA_CHUNK * GLA_STEP_CHUNKS
    nsteps = L // R
    rb0 = row_off // R
    has_init = s0 is not None
    has_final = ns_shape is not None
    fwd = lambda b, h, c: rb0 + b * nsteps + c
    bwd = lambda b, h, c: rb0 + b * nsteps + nsteps - 1 - c
    whole = lambda shape: pl.BlockSpec(shape, lambda b, h, c: (0,) * len(shape))

    def specs(rowmap):
        return [pl.BlockSpec((R, HK), lambda b, h, c: (rowmap(b, h, c), col_q + h)),
                pl.BlockSpec((R, HK), lambda b, h, c: (rowmap(b, h, c), col_k + h)),
                pl.BlockSpec((R, HV), lambda b, h, c: (rowmap(b, h, c), col_v + h)),
                pl.BlockSpec((R, LANES), lambda b, h, c: (rowmap(b, h, c), 0))]

    in_specs = specs(fwd) + specs(bwd) + [
        pl.BlockSpec((2, LANES, HK), lambda b, h, c: (0, 0, h)),
        pl.BlockSpec((2, LANES, HK), lambda b, h, c: (0, 0, h)),
        pl.BlockSpec((2, 1, HK), lambda b, h, c: (0, 0, h)),
        whole((1, HV)),
        pl.BlockSpec((L, HV), lambda b, h, c: (row_off // L + b, col_g + h)),
        whole((2, R, R)),
        whole((2, R, R)),
    ]
    args = [proj, proj, proj, z, proj, proj, proj, z, w2h, w2l, b2, gla_norm.reshape(1, HV), proj, tri, code]
    if has_init:
        in_specs.append(pl.BlockSpec((None, None, 2, None, HK, HV), lambda b, h, c: (b, s0_layer, 0, h, 0, 0)))
        args.append(s0)
    aliases = {}
    out_shape = [jax.ShapeDtypeStruct((t, H * HV), BF16)]
    out_specs = [pl.BlockSpec((L, HV), lambda b, h, c: (row_off // L + b, h))]
    if o_prev is not None:
        aliases[len(args)] = 0
        in_specs.append(pl.BlockSpec(memory_space=pl.ANY))
        args.append(o_prev)
    if has_final:
        out_shape.append(jax.ShapeDtypeStruct(ns_shape, F32))
        out_specs.append(pl.BlockSpec((None, None, 2, None, HK, HV), lambda b, h, c: (b, ns_layer, 0, h, 0, 0)))
        if ns_prev is not None:
            aliases[len(args)] = 1
            in_specs.append(pl.BlockSpec(memory_space=pl.ANY))
            args.append(ns_prev)
    return pl.pallas_call(
        functools.partial(_gla_kernel, nsteps=nsteps, scale=float(HK) ** -0.5,
                          has_init=has_init, n_aliased=len(aliases), has_final=has_final),
        out_shape=tuple(out_shape),
        grid=(n_seq, H, nsteps),
        in_specs=in_specs,
        out_specs=tuple(out_specs),
        scratch_shapes=[pltpu.VMEM((L, HV), F32), pltpu.VMEM((HV, HK), F32), pltpu.VMEM((HV, HK), F32)],
        input_output_aliases=aliases,
        compiler_params=_cp("arbitrary", "arbitrary", "arbitrary"),
        name="gla_init" if has_init else "gla_zero",
    )(*args)


def _branch_kernel(u_ref, og_ref, ga_ref, gb_ref, wc_ref, wg_ref, m_ref):
    a = _dot(u_ref[...], wc_ref[...])
    b = _dot(og_ref[...], wg_ref[...])
    m = _sigmoid(ga_ref[...].astype(F32)) * a + _sigmoid(gb_ref[...].astype(F32)) * b
    m_ref[...] = m.astype(m_ref.dtype)


def _branches(u_act, og, proj, w_conv_out, w_gla_out, col_ga, col_gb):
    t, c = u_act.shape
    d = w_conv_out.shape[1]
    dv = w_gla_out.shape[0]
    bm = 512
    return pl.pallas_call(
        _branch_kernel,
        out_shape=jax.ShapeDtypeStruct((t, d), BF16),
        grid=(t // bm,),
        in_specs=[
            pl.BlockSpec((bm, c), lambda i: (i, 0)),
            pl.BlockSpec((bm, dv), lambda i: (i, 0)),
            pl.BlockSpec((bm, d), lambda i: (i, col_ga)),
            pl.BlockSpec((bm, d), lambda i: (i, col_gb)),
            pl.BlockSpec((c, d), lambda i: (0, 0)),
            pl.BlockSpec((dv, d), lambda i: (0, 0)),
        ],
        out_specs=pl.BlockSpec((bm, d), lambda i: (i, 0)),
        compiler_params=_cp("arbitrary"),
        name="branches",
    )(u_act, og, proj, proj, w_conv_out, w_gla_out)


def _mix_out_kernel(m_ref, w_ref, x_ref, g1_ref, nw_ref, sh_ref, sc_ref, wrh_ref, wrl_ref, xo_ref, h_ref, lg_ref):
    x = x_ref[...] + g1_ref[...] * _dot(m_ref[...], w_ref[...])
    xo_ref[...] = x
    h = (_rms(x) * nw_ref[...]) * (1.0 + sc_ref[...]) + sh_ref[...]
    hh, hl = _split2(h)
    h_ref[...] = hh
    lg_ref[...] = _dot_nt(wrl_ref[...], hh) + _dot_nt(wrh_ref[...], hl) + _dot_nt(wrh_ref[...], hh)


def _mix_out(merged, w_mix, x, mods, norm_w, wr_hi, wr_lo, S):
    t, d = x.shape
    e = wr_hi.shape[0]
    bm = 512
    seg = lambda i: (i * bm) // S
    mod = lambda which: pl.BlockSpec((None, 1, d), lambda i: (seg(i) * N_MOD + which, 0, 0))
    return pl.pallas_call(
        _mix_out_kernel,
        out_shape=(jax.ShapeDtypeStruct((t, d), F32), jax.ShapeDtypeStruct((t, d), BF16),
                   jax.ShapeDtypeStruct((e, t), F32)),
        grid=(t // bm,),
        in_specs=[
            pl.BlockSpec((bm, d), lambda i: (i, 0)),
            pl.BlockSpec((d, d), lambda i: (0, 0)),
            pl.BlockSpec((bm, d), lambda i: (i, 0)),
            mod(2),
            pl.BlockSpec((1, d), lambda i: (0, 0)),
            mod(3), mod(4),
            pl.BlockSpec((e, d), lambda i: (0, 0)),
            pl.BlockSpec((e, d), lambda i: (0, 0)),
        ],
        out_specs=(pl.BlockSpec((bm, d), lambda i: (i, 0)), pl.BlockSpec((bm, d), lambda i: (i, 0)),
                   pl.BlockSpec((e, bm), lambda i: (0, i))),
        compiler_params=_cp("arbitrary"),
        name="mix_out",
    )(merged, w_mix, x, mods, norm_w, mods, mods, wr_hi, wr_lo)


def _route_kernel(lg_ref, slot_ref, gate_ref, cnt_ref, *, n_sets, L, cap):
    lg = lg_ref[...]
    e = lg.shape[0]
    ex = jnp.exp(lg - jnp.max(lg, axis=0, keepdims=True))
    aff = ex / jnp.sum(ex, axis=0, keepdims=True)
    bits = pltpu.bitcast(aff, I32)
    sets = [bits[:, s * L:(s + 1) * L] for s in range(n_sets)]

    def body(it, vs):
        bit = jnp.left_shift(jnp.int32(1), 30 - it)
        out = []
        for a, v in zip(sets, vs):
            cand = v | bit
            cnt = jnp.sum(jnp.where(a >= cand, 1.0, 0.0), axis=1, keepdims=True)
            out.append(jnp.where(cnt >= cap, cand, v))
        return tuple(out)

    thr = lax.fori_loop(0, 31, body, tuple(jnp.zeros((e, 1), I32) for _ in range(n_sets)))

    pb = ROUTE_BLOCK
    r = lax.broadcasted_iota(I32, (pb, pb), 0)
    c = lax.broadcasted_iota(I32, (pb, pb), 1)
    tri = jnp.where(r <= c, 1.0, 0.0).astype(BF16)
    cnt_ref[...] = jnp.full(cnt_ref.shape, n_sets * cap, I32)
    for s in range(n_sets):
        a, v = sets[s], thr[s]
        n_gt = jnp.sum(jnp.where(a > v, 1.0, 0.0), axis=1, keepdims=True)
        need = cap - n_gt
        carry_eq = jnp.zeros((e, 1), F32)
        carry_sel = jnp.zeros((e, 1), F32)
        for b0 in range(0, L, pb):
            blk = (s * L + b0) // pb
            cnt_ref[:, blk:blk + 1] = carry_sel.astype(I32) + s * cap
            ab = a[:, b0:b0 + pb]
            eq = jnp.where(ab == v, 1.0, 0.0)
            rank_eq = _dot(eq.astype(BF16), tri) - eq + carry_eq
            sel = jnp.where(ab > v, 1.0, jnp.where(rank_eq < need, eq, 0.0))
            rank_sel = _dot(sel.astype(BF16), tri) - sel + carry_sel
            carry_eq = carry_eq + jnp.sum(eq, axis=1, keepdims=True)
            carry_sel = carry_sel + jnp.sum(sel, axis=1, keepdims=True)
            cols = slice(s * L + b0, s * L + b0 + pb)
            slot_ref[:, cols] = jnp.where(sel > 0.0, rank_sel.astype(I32) + s * cap, -1)
            gate_ref[:, cols] = jnp.where(sel > 0.0, aff[:, cols], 0.0)


def _route(logits_t, *, col0, n_groups, n_sets, L, cap):
    e = logits_t.shape[0]
    n = n_sets * L
    assert n // ROUTE_BLOCK < LANES
    return pl.pallas_call(
        functools.partial(_route_kernel, n_sets=n_sets, L=L, cap=cap),
        out_shape=(jax.ShapeDtypeStruct((e, n_groups * n), I32), jax.ShapeDtypeStruct((e, n_groups * n), F32),
                   jax.ShapeDtypeStruct((e, n_groups * LANES), I32)),
        grid=(n_groups,),
        in_specs=[pl.BlockSpec((e, n), lambda g: (0, col0 + g))],
        out_specs=(pl.BlockSpec((e, n), lambda g: (0, g)), pl.BlockSpec((e, n), lambda g: (0, g)),
                   pl.BlockSpec((e, LANES), lambda g: (0, g))),
        compiler_params=_cp("arbitrary"),
        name="route",
    )(logits_t)


GATHER_TOKENS = 512
GATHER_WINDOW = 128
GATHER_GROUP = 2
SCATTER_TOKENS = 512
SCATTER_WINDOW = 128
BF16_ROWS = 16


def _gather_kernel(cnt_ref, h_ref, slot_ref, gate_ref, g_ref, gs_ref, acc, gs_acc, *, nslot, n_exp, n_cnt):
    s_tok = h_ref.shape[0]
    tk = min(GATHER_TOKENS, s_tok)
    w = min(GATHER_WINDOW, nslot)
    grp = acc.shape[0]
    seg, g0 = pl.program_id(0), pl.program_id(1) * grp
    acc[...] = jnp.zeros_like(acc)
    gs_acc[...] = jnp.zeros_like(gs_acc)

    def onehot(el, c, start, lower=None):
        ids = lax.broadcasted_iota(I32, (w, tk), 0) + start
        hit = slot_ref[el, :, c * tk:(c + 1) * tk] == ids
        oh = jnp.where(hit, 1.0, 0.0)
        gate = jnp.where(hit, gate_ref[el, :, c * tk:(c + 1) * tk], 0.0)
        if lower is not None:
            oh = jnp.where(ids >= lower, oh, 0.0)
            gate = jnp.where(ids >= lower, gate, 0.0)
        return oh.astype(BF16), jnp.sum(gate, axis=1, keepdims=True)

    for c in range(s_tok // tk):
        starts, his, pieces = [], [], []
        for el in range(grp):
            base = (seg * n_exp + g0 + el) * n_cnt
            lo = cnt_ref[base + c * (tk // ROUTE_BLOCK)]
            his.append(cnt_ref[base + (c + 1) * (tk // ROUTE_BLOCK)])
            starts.append(pl.multiple_of(jnp.minimum((lo // SUBLANES) * SUBLANES, nslot - w), SUBLANES))
            oh, gsum = onehot(el, c, starts[el])
            pieces.append(oh)
            gs_acc[el, pl.ds(starts[el], w), :] += gsum
        part = _dot(jnp.concatenate(pieces, axis=0), h_ref[c * tk:(c + 1) * tk, :])
        for el in range(grp):
            acc[el, pl.ds(starts[el], w), :] += part[el * w:(el + 1) * w, :]

        for el in range(grp):
            n_extra = (jnp.maximum(his[el] - starts[el], 1) - 1) // w

            def body(p, carry, el=el, c=c, start=starts[el]):
                lower = start + (p + 1) * w
                st = pl.multiple_of(jnp.minimum(lower, nslot - w), SUBLANES)
                oh, gsum = onehot(el, c, st, lower)
                acc[el, pl.ds(st, w), :] += _dot(oh, h_ref[c * tk:(c + 1) * tk, :])
                gs_acc[el, pl.ds(st, w), :] += gsum
                return carry

            lax.fori_loop(0, n_extra, body, 0)
    g_ref[...] = acc[...].astype(g_ref.dtype)
    gs_ref[...] = gs_acc[...]


def _gather(cnt, h2, slot3, gate3, S, nslot, n_cnt):
    t, d = h2.shape
    e = slot3.shape[0]
    nseg = t // S
    grp = GATHER_GROUP
    assert e % grp == 0 and nslot % SUBLANES == 0
    return pl.pallas_call(
        functools.partial(_gather_kernel, nslot=nslot, n_exp=e, n_cnt=n_cnt),
        out_shape=(jax.ShapeDtypeStruct((e, nseg * nslot, d), BF16), jax.ShapeDtypeStruct((e, nseg * nslot, 1), F32)),
        grid_spec=pltpu.PrefetchScalarGridSpec(
            num_scalar_prefetch=1,
            grid=(nseg, e // grp),
            in_specs=[
                pl.BlockSpec((S, d), lambda s, x, cnt: (s, 0), pipeline_mode=pl.Buffered(1)),
                pl.BlockSpec((grp, 1, S), lambda s, x, cnt: (x, 0, s)),
                pl.BlockSpec((grp, 1, S), lambda s, x, cnt: (x, 0, s)),
            ],
            out_specs=(pl.BlockSpec((grp, nslot, d), lambda s, x, cnt: (x, s, 0)),
                       pl.BlockSpec((grp, nslot, 1), lambda s, x, cnt: (x, s, 0))),
            scratch_shapes=[pltpu.VMEM((grp, nslot, d), F32), pltpu.VMEM((grp, nslot, 1), F32)],
        ),
        compiler_params=_cp("arbitrary", "arbitrary"),
        name="moe_gather",
    )(cnt, h2, slot3, gate3)


def _ffn_up_kernel(g_ref, wg_ref, wu_ref, a_ref):
    g = g_ref[...]
    a = _dot(g, wg_ref[...].astype(BF16))
    b = _dot(g, wu_ref[...].astype(BF16))
    a_ref[...] = (_silu(a) * b).astype(a_ref.dtype)


def _ffn_up(gathered, w_gate, w_up, layer):
    e, m, d = gathered.shape
    f = w_gate.shape[3]
    bf = 512
    return pl.pallas_call(
        _ffn_up_kernel,
        out_shape=jax.ShapeDtypeStruct((e, m, f), BF16),
        grid=(e, f // bf),
        in_specs=[
            pl.BlockSpec((None, m, d), lambda x, j: (x, 0, 0)),
            pl.BlockSpec((None, None, d, bf), lambda x, j: (layer, x, 0, j)),
            pl.BlockSpec((None, None, d, bf), lambda x, j: (layer, x, 0, j)),
        ],
        out_specs=pl.BlockSpec((None, m, bf), lambda x, j: (x, 0, j)),
        compiler_params=_cp("arbitrary", "arbitrary"),
        name="ffn_up",
    )(gathered, w_gate, w_up)


def _ffn_down_kernel(a_ref, wd_ref, gs_ref, y_ref):
    y = _dot(a_ref[...], wd_ref[...].astype(BF16))
    y_ref[...] = (y * gs_ref[...]).astype(y_ref.dtype)


def _ffn_down(act, w_down, gslot, layer):
    e, m, f = act.shape
    d = w_down.shape[3]
    bn = 512
    return pl.pallas_call(
        _ffn_down_kernel,
        out_shape=jax.ShapeDtypeStruct((e, m, d), BF16),
        grid=(e, d // bn),
        in_specs=[
            pl.BlockSpec((None, m, f), lambda x, j: (x, 0, 0)),
            pl.BlockSpec((None, None, f, bn), lambda x, j: (layer, x, 0, j)),
            pl.BlockSpec((None, m, 1), lambda x, j: (x, 0, 0)),
        ],
        out_specs=pl.BlockSpec((None, m, bn), lambda x, j: (x, 0, j)),
        compiler_params=_cp("arbitrary", "arbitrary"),
        name="ffn_down",
    )(act, w_down, gslot)


def _scatter_kernel(cnt_ref, slot_ref, x_ref, g2_ref, y_hbm, o_ref, ycat, ybuf, sems, sem_extra,
                    *, nslot, n_exp, n_cnt, S):
    i = pl.program_id(0)
    tb = o_ref.shape[0]
    w = min(SCATTER_WINDOW, nslot)
    half = n_exp // 2
    cur = lax.rem(i, 2)

    def bounds(step, e):
        seg = (step * tb) // S
        base = (seg * n_exp + e) * n_cnt + (step * tb - seg * S) // ROUTE_BLOCK
        lo = cnt_ref[base]
        start = pl.multiple_of(jnp.minimum((lo // BF16_ROWS) * BF16_ROWS, nslot - w), BF16_ROWS)
        return seg * nslot, start, cnt_ref[base + tb // ROUTE_BLOCK]

    def window(e, row0, start, dst, sem):
        return pltpu.make_async_copy(y_hbm.at[e, pl.ds(row0 + start, w), :], dst, sem)

    def main_window(step, buf, e):
        row0, start, _ = bounds(step, e)
        return window(e, row0, start, ycat.at[buf, pl.ds(e * w, w), :], sems.at[buf, e])

    @pl.when(i == 0)
    def _():
        for e in range(n_exp):
            main_window(0, 0, e).start()

    @pl.when(i + 1 < pl.num_programs(0))
    def _():
        for e in range(n_exp):
            main_window(i + 1, 1 - cur, e).start()

    row0, _, _ = bounds(i, 0)
    starts = [bounds(i, e)[1] for e in range(n_exp)]
    his = [bounds(i, e)[2] for e in range(n_exp)]

    def onehot(e, start, lower=None):
        ids = lax.broadcasted_iota(I32, (w, tb), 0) + start
        oh = jnp.where(slot_ref[e] == ids, 1.0, 0.0)
        if lower is not None:
            oh = jnp.where(ids >= lower, oh, 0.0)
        return oh.astype(BF16)

    for g in range(2):
        oh = jnp.concatenate([onehot(e, starts[e]) for e in range(g * half, (g + 1) * half)], axis=0)
        for e in range(g * half, (g + 1) * half):
            main_window(i, cur, e).wait()
        part = _dot_tn(oh, ycat[cur, g * half * w:(g + 1) * half * w, :])
        if g == 0:
            o_ref[...] = part
        else:
            o_ref[...] += part

    for e in range(n_exp):
        n_extra = (jnp.maximum(his[e] - starts[e], 1) - 1) // w

        def body(p, carry, e=e):
            lower = starts[e] + (p + 1) * w
            st = pl.multiple_of(jnp.minimum(lower, nslot - w), BF16_ROWS)
            cp = window(e, row0, st, ybuf, sem_extra.at[0])
            cp.start()
            cp.wait()
            o_ref[...] += _dot_tn(onehot(e, st, lower), ybuf[...])
            return carry

        lax.fori_loop(0, n_extra, body, 0)

    o_ref[...] = x_ref[...] + g2_ref[...] * o_ref[...]


def _scatter(cnt, slot3, yg, x, mods, S, nslot, n_cnt):
    t, d = x.shape
    e = slot3.shape[0]
    tb = min(SCATTER_TOKENS, S)
    w = min(SCATTER_WINDOW, nslot)
    assert e % 2 == 0 and nslot % BF16_ROWS == 0
    seg = lambda i: (i * tb) // S
    return pl.pallas_call(
        functools.partial(_scatter_kernel, nslot=nslot, n_exp=e, n_cnt=n_cnt, S=S),
        out_shape=jax.ShapeDtypeStruct((t, d), F32),
        grid_spec=pltpu.PrefetchScalarGridSpec(
            num_scalar_prefetch=1,
            grid=(t // tb,),
            in_specs=[
                pl.BlockSpec((e, 1, tb), lambda i, cnt: (0, 0, i)),
                pl.BlockSpec((tb, d), lambda i, cnt: (i, 0)),
                pl.BlockSpec((None, 1, d), lambda i, cnt: (seg(i) * N_MOD + 5, 0, 0)),
                pl.BlockSpec(memory_space=pl.ANY),
            ],
            out_specs=pl.BlockSpec((tb, d), lambda i, cnt: (i, 0)),
            scratch_shapes=[pltpu.VMEM((2, e * w, d), BF16), pltpu.VMEM((w, d), BF16),
                            pltpu.SemaphoreType.DMA((2, e)), pltpu.SemaphoreType.DMA((1,))],
        ),
        compiler_params=_cp("arbitrary"),
        name="moe_scatter",
    )(cnt, slot3, x, mods, yg)


def _final_norm_kernel(x_ref, w_ref, o_ref):
    o_ref[...] = _rms(x_ref[...]) * w_ref[...]


def _final_norm(x, w, row_off, rows):
    d = x.shape[1]
    bm = 512
    off = row_off // bm
    return pl.pallas_call(
        _final_norm_kernel,
        out_shape=jax.ShapeDtypeStruct((rows, d), F32),
        grid=(rows // bm,),
        in_specs=[pl.BlockSpec((bm, d), lambda i: (off + i, 0)), pl.BlockSpec((1, d), lambda i: (0, 0))],
        out_specs=pl.BlockSpec((bm, d), lambda i: (i, 0)),
        compiler_params=_cp("arbitrary"),
        name="final_norm",
    )(x, w)


def kernel(x_prompt, x_sample, state_gla, c, c_ctx, w_in, conv_w, conv_b, conv_norm, w_conv_out, gla_gate_w2, gla_gate_b, gla_norm, w_gla_out, w_mix_out, w_ada, b_ada, norm_mix, norm_ffn, w_router, w_exp_gate, w_exp_up, w_exp_down, norm_final):
    B, SEQ, D = x_prompt.shape
    DB, DSEQ, _ = x_sample.shape
    DEPTH = w_in.shape[0]
    H, HK, HV = state_gla.shape[3:]
    DC = conv_w.shape[2]
    DK, DV = H * HK, H * HV
    RANK = gla_gate_w2.shape[2]
    E = w_router.shape[2]
    S = DSEQ
    R = GLA_CHUNK * GLA_STEP_CHUNKS
    assert B * SEQ == S, "prompt tokens must fill exactly one segment"
    assert 2 * RANK <= LANES and conv_w.shape[1] // 2 <= CONV_HALO
    assert SEQ % R == 0 and DSEQ % R == 0 and SEQ % ROUTE_BLOCK == 0 and S % GATHER_TOKENS == 0
    nseg = 1 + DB
    assert nseg <= 8
    T = nseg * S
    cap_p = (EC_CAPACITY_FACTOR * SEQ) // E
    cap_s = (EC_CAPACITY_FACTOR * DSEQ) // E
    nslot = B * cap_p
    assert nslot == cap_s
    n_cnt = S // ROUTE_BLOCK + 1

    x = _embed(x_prompt.reshape(B * SEQ, D), x_sample.reshape(DB * DSEQ, D), _pos_table(DSEQ, D), S)

    cond8 = jnp.zeros((8, D), F32).at[0].set(c_ctx).at[1:1 + DB].set(c)
    mods_all = _adaln(cond8, w_ada, b_ada).reshape(DEPTH, 8 * N_MOD, 1, D)

    z0 = 2 * DC + 2 * DK + 2 * DV
    col_q, col_k = (2 * DC) // HK, (2 * DC + DK) // HK
    col_v = (2 * DC + 2 * DK) // HV
    col_g = (2 * DC + 2 * DK + DV) // HV
    col_ga, col_gb = z0 // D, z0 // D + 1
    tri, code = _gla_masks()

    new_state = None
    ns_shape = (B, DEPTH, 2, H, HK, HV)
    for l in range(DEPTH):
        mods = mods_all[l]
        w_b = w_in[l, :, z0 + 2 * RANK:].astype(BF16)
        w_z = jnp.pad(w_in[l, :, z0:z0 + 2 * RANK], ((0, 0), (0, LANES - 2 * RANK))).astype(BF16)
        proj, z = _in_proj(x, norm_mix[l].reshape(1, D), mods, w_in, l, z0, w_b, w_z, S)

        u_act = _conv_branch(proj, conv_w[l], conv_b[l], conv_norm[l], S, SEQ, DSEQ)

        w2pad = jnp.zeros((2, LANES, DK), F32)
        w2pad = w2pad.at[0, :RANK].set(gla_gate_w2[l, 0]).at[1, RANK:2 * RANK].set(gla_gate_w2[l, 1])
        w2h = w2pad.astype(BF16)
        w2l = (w2pad - w2h.astype(F32)).astype(BF16)
        b2 = gla_gate_b[l].reshape(2, 1, DK)
        common = dict(H=H, HK=HK, HV=HV, col_q=col_q, col_k=col_k, col_v=col_v, col_g=col_g)
        og, new_state = _gla(proj, z, w2h, w2l, b2, gla_norm[l], tri, code, row_off=0, n_seq=B, L=SEQ,
                             ns_prev=new_state, ns_shape=ns_shape, ns_layer=l, **common)
        (og,) = _gla(proj, z, w2h, w2l, b2, gla_norm[l], tri, code, row_off=S, n_seq=DB, L=DSEQ,
                     s0=state_gla, s0_layer=l, o_prev=og, **common)

        merged = _branches(u_act, og, proj, w_conv_out[l].astype(BF16), w_gla_out[l].astype(BF16), col_ga, col_gb)
        wr_t = w_router[l].T
        wr_hi = wr_t.astype(BF16)
        wr_lo = (wr_t - wr_hi.astype(F32)).astype(BF16)
        x, h2, logits_t = _mix_out(merged, w_mix_out[l].astype(BF16), x, mods, norm_ffn[l].reshape(1, D),
                                   wr_hi, wr_lo, S)

        slot_p, gate_p, cnt_p = _route(logits_t, col0=0, n_groups=1, n_sets=B, L=SEQ, cap=cap_p)
        slot_s, gate_s, cnt_s = _route(logits_t, col0=1, n_groups=DB, n_sets=1, L=DSEQ, cap=cap_s)
        slot3 = jnp.concatenate([slot_p, slot_s], axis=1).reshape(E, 1, T)
        gate3 = jnp.concatenate([gate_p, gate_s], axis=1).reshape(E, 1, T)
        cnt = jnp.concatenate([cnt_p, cnt_s], axis=1).reshape(E, nseg, LANES)[:, :, :n_cnt]
        cnt = cnt.transpose(1, 0, 2).reshape(nseg * E * n_cnt)

        gathered, gslot = _gather(cnt, h2, slot3, gate3, S, nslot, n_cnt)
        act = _ffn_up(gathered, w_exp_gate, w_exp_up, l)
        yg = _ffn_down(act, w_exp_down, gslot, l)
        x = _scatter(cnt, slot3, yg, x, mods, S, nslot, n_cnt)

    nf = norm_final.reshape(1, D)
    y_prompt = _final_norm(x, nf, 0, B * SEQ).reshape(B, SEQ, D)
    y_sample = _final_norm(x, nf, S, DB * DSEQ).reshape(DB, DSEQ, D)
    return (y_prompt, y_sample, new_state)
```

```python
import functools

import jax
import jax.numpy as jnp
import numpy as np
from jax import lax
from jax.experimental import pallas as pl
from jax.experimental.pallas import tpu as pltpu

F32 = jnp.float32
BF16 = jnp.bfloat16
I32 = jnp.int32
HI = lax.Precision.HIGHEST

EPS = 1e-6
GRID_W = 64
POS_THETA = 10000.0
GLA_CHUNK = 64
GLA_STEP_CHUNKS = 4
GLA_TAU = 16.0
EC_CAPACITY_FACTOR = 2
N_MOD = 6
CONV_HALO = 16
SUBLANES = 8
LANES = 128
ROUTE_BLOCK = 256
VMEM_LIMIT = 56 * 1024 * 1024

NT_DIMS = (((1,), (1,)), ((), ()))
TN_DIMS = (((0,), (0,)), ((), ()))


def _cp(*sem):
    return pltpu.CompilerParams(dimension_semantics=sem, vmem_limit_bytes=VMEM_LIMIT)


def _sigmoid(x):
    return 1.0 / (1.0 + jnp.exp(-x))


def _silu(x):
    return x * _sigmoid(x)


def _rms(x):
    return x * lax.rsqrt(jnp.mean(x * x, axis=-1, keepdims=True) + EPS)


def _dot(a, b):
    return jnp.dot(a, b, preferred_element_type=F32)


def _dot_nt(a, b):
    return lax.dot_general(a, b, NT_DIMS, preferred_element_type=F32)


def _dot_tn(a, b):
    return lax.dot_general(a, b, TN_DIMS, preferred_element_type=F32)


def _split2(x):
    hi = x.astype(BF16)
    return hi, (x - hi.astype(F32)).astype(BF16)


def _split3(x):
    hi = x.astype(BF16)
    r = x - hi.astype(F32)
    mid = r.astype(BF16)
    return hi, mid, (r - mid.astype(F32)).astype(BF16)


def _embed_kernel(xp_ref, xs_ref, emb_ref, o_ref, *, nb_p):
    i = pl.program_id(0)

    @pl.when(i < nb_p)
    def _():
        o_ref[...] = xp_ref[...]

    @pl.when(i >= nb_p)
    def _():
        o_ref[...] = xs_ref[...] + emb_ref[...]


def _pos_table(n, d):
    rows = n // GRID_W
    rr, cc = jnp.meshgrid(jnp.arange(rows, dtype=F32), jnp.arange(GRID_W, dtype=F32), indexing="ij")
    rr = rr.reshape(-1, 1)
    cc = cc.reshape(-1, 1)
    quarter = d // 4
    freqs = POS_THETA ** (-jnp.arange(quarter, dtype=F32) / quarter)
    return jnp.concatenate([jnp.sin(rr * freqs), jnp.cos(rr * freqs), jnp.sin(cc * freqs), jnp.cos(cc * freqs)], axis=-1)


def _embed(xp2, xs2, emb, S):
    tp, d = xp2.shape
    ts = xs2.shape[0]
    bm = min(512, S)
    nb_p, nb_seq = tp // bm, S // bm
    return pl.pallas_call(
        functools.partial(_embed_kernel, nb_p=nb_p),
        out_shape=jax.ShapeDtypeStruct((tp + ts, d), F32),
        grid=((tp + ts) // bm,),
        in_specs=[
            pl.BlockSpec((bm, d), lambda i: (jnp.minimum(i, nb_p - 1), 0)),
            pl.BlockSpec((bm, d), lambda i: (jnp.maximum(i - nb_p, 0), 0)),
            pl.BlockSpec((bm, d), lambda i: (jnp.maximum(i - nb_p, 0) % nb_seq, 0)),
        ],
        out_specs=pl.BlockSpec((bm, d), lambda i: (i, 0)),
        compiler_params=_cp("arbitrary"),
        name="embed",
    )(xp2, xs2, emb)


def _ada_kernel(c_ref, w_ref, b_ref, o_ref):
    s = _silu(c_ref[...])
    o_ref[0] = jnp.dot(s, w_ref[0], precision=HI, preferred_element_type=F32) + b_ref[0]


def _adaln(cond8, w_ada, b_ada):
    depth, d, n = w_ada.shape
    bn = 1024
    return pl.pallas_call(
        _ada_kernel,
        out_shape=jax.ShapeDtypeStruct((depth, 8, n), F32),
        grid=(depth, n // bn),
        in_specs=[
            pl.BlockSpec((8, d), lambda l, j: (0, 0)),
            pl.BlockSpec((1, d, bn), lambda l, j: (l, 0, j)),
            pl.BlockSpec((1, 1, bn), lambda l, j: (l, 0, j)),
        ],
        out_specs=pl.BlockSpec((1, 8, bn), lambda l, j: (l, 0, j)),
        compiler_params=_cp("arbitrary", "arbitrary"),
        name="adaln",
    )(cond8, w_ada, b_ada.reshape(depth, 1, n))


def _in_proj_kernel(x_ref, nw_ref, sh_ref, sc_ref, wa_ref, wb_ref, wz_ref, o_ref, z_ref, h_scr, *, na, skip):
    j = pl.program_id(1)

    @pl.when(j == 0)
    def _():
        h = (_rms(x_ref[...]) * nw_ref[...]) * (1.0 + sc_ref[...]) + sh_ref[...]
        hb = h.astype(BF16)
        h_scr[...] = hb
        z_ref[...] = _dot_nt(hb, wz_ref[...].astype(BF16))

    @pl.when(j < na)
    def _():
        o_ref[...] = _dot_nt(h_scr[...], wa_ref[...].astype(BF16)).astype(o_ref.dtype)

    @pl.when(j >= na)
    def _():
        w = jnp.concatenate([wa_ref[skip:, :], wb_ref[...]], axis=0).astype(BF16)
        o_ref[...] = _dot_nt(h_scr[...], w).astype(o_ref.dtype)


def _in_proj(x, norm_w, mods, w_t, layer, z0, skip, S):
    t, d = x.shape
    bm, bn = min(1024, S), 1024
    n_rows = w_t.shape[1]
    assert z0 % bn == 0 and skip % SUBLANES == 0 and bn % skip == 0 and (n_rows - skip) % bn == 0
    na, nj = z0 // bn, (n_rows - skip) // bn
    seg = lambda i: (i * bm) // S
    return pl.pallas_call(
        functools.partial(_in_proj_kernel, na=na, skip=skip),
        out_shape=(jax.ShapeDtypeStruct((t, nj * bn), BF16), jax.ShapeDtypeStruct((t, LANES), F32)),
        grid=(t // bm, nj),
        in_specs=[
            pl.BlockSpec((bm, d), lambda i, j: (i, 0), pipeline_mode=pl.Buffered(1)),
            pl.BlockSpec((1, d), lambda i, j: (0, 0)),
            pl.BlockSpec((None, 1, d), lambda i, j: (seg(i) * N_MOD + 0, 0, 0)),
            pl.BlockSpec((None, 1, d), lambda i, j: (seg(i) * N_MOD + 1, 0, 0)),
            pl.BlockSpec((None, bn, d), lambda i, j: (layer, j, 0)),
            pl.BlockSpec((None, skip, d), lambda i, j: (layer, (bn // skip) * (jnp.maximum(j, na) + 1), 0)),
            pl.BlockSpec((None, LANES, d), lambda i, j: (layer, z0 // LANES, 0)),
        ],
        out_specs=(pl.BlockSpec((bm, bn), lambda i, j: (i, j)), pl.BlockSpec((bm, LANES), lambda i, j: (i, 0))),
        scratch_shapes=[pltpu.VMEM((bm, d), BF16)],
        compiler_params=_cp("arbitrary", "arbitrary"),
        name="in_proj",
    )(x, norm_w, mods, mods, w_t, w_t, w_t)


CONV_TR, CONV_TC = 64, 256


def _conv_kernel(uv, ug, uvp, ugp, uvn, ugn, cw_ref, cb_ref, cn_ref, o_ref, buf, cv, *, tb, S, Lp, Ls, K):
    row0 = pl.program_id(0) * tb
    pos = jnp.where(row0 < S, lax.rem(row0, Lp), lax.rem(row0 - S, Ls))
    seq_len = jnp.where(row0 < S, Lp, Ls)
    at_start = pos == 0
    at_end = pos + tb == seq_len
    c = uv.shape[1]
    pad = K // 2

    def glu(a, b):
        return a[...].astype(F32) * _sigmoid(b[...].astype(F32))

    rows = tb + 2 * CONV_HALO
    buf[0, 0:CONV_HALO, :] = jnp.where(at_start, 0.0, glu(uvp, ugp))
    buf[0, CONV_HALO:CONV_HALO + tb, :] = glu(uv, ug)
    buf[0, CONV_HALO + tb:, :] = jnp.where(at_end, 0.0, glu(uvn, ugn))
    for p in range(1, SUBLANES):
        buf[p, 0:rows - SUBLANES, :] = buf[0, p:p + rows - SUBLANES, :]

    for r0 in range(0, tb, CONV_TR):
        for c0 in range(0, c, CONV_TC):
            acc = jnp.broadcast_to(cb_ref[:, c0:c0 + CONV_TC], (CONV_TR, CONV_TC))
            for k in range(K):
                start = CONV_HALO - pad + k
                p = start % SUBLANES
                a = start - p + r0
                acc = acc + cw_ref[k:k + 1, c0:c0 + CONV_TC] * buf[p, a:a + CONV_TR, c0:c0 + CONV_TC]
            cv[r0:r0 + CONV_TR, c0:c0 + CONV_TC] = acc

    y = _rms(cv[...]) * cn_ref[...]
    o_ref[...] = _silu(y).astype(o_ref.dtype)


def _conv_branch(proj, conv_w, conv_b, conv_norm, S, Lp, Ls):
    t = proj.shape[0]
    k, c = conv_w.shape
    tb = 256
    nh = tb // CONV_HALO
    last = t // CONV_HALO - 1
    main = lambda col: pl.BlockSpec((tb, c), lambda i: (i, col))
    prev = lambda col: pl.BlockSpec((CONV_HALO, c), lambda i: (jnp.maximum(i * nh - 1, 0), col))
    nxt = lambda col: pl.BlockSpec((CONV_HALO, c), lambda i: (jnp.minimum((i + 1) * nh, last), col))
    return pl.pallas_call(
        functools.partial(_conv_kernel, tb=tb, S=S, Lp=Lp, Ls=Ls, K=k),
        out_shape=jax.ShapeDtypeStruct((t, c), BF16),
        grid=(t // tb,),
        in_specs=[main(0), main(1), prev(0), prev(1), nxt(0), nxt(1),
                  pl.BlockSpec((k, c), lambda i: (0, 0)),
                  pl.BlockSpec((1, c), lambda i: (0, 0)),
                  pl.BlockSpec((1, c), lambda i: (0, 0))],
        out_specs=pl.BlockSpec((tb, c), lambda i: (i, 0)),
        scratch_shapes=[pltpu.VMEM((SUBLANES, tb + 2 * CONV_HALO, c), F32), pltpu.VMEM((tb, c), F32)],
        compiler_params=_cp("arbitrary"),
        name="conv_branch",
    )(proj, proj, proj, proj, proj, proj, conv_w, conv_b.reshape(1, c), conv_norm.reshape(1, c))


def _log_sigmoid(x):
    return jnp.minimum(x, 0.0) - jnp.log(1.0 + jnp.exp(-jnp.abs(x)))


def _gla_masks():
    C, n = GLA_CHUNK, GLA_STEP_CHUNKS
    R = C * n
    r = np.arange(R)[:, None]
    s = np.arange(R)[None, :]
    tris, codes = [], []
    for reverse in (False, True):
        sr, sc = r // C, s // C
        before_eq = s <= r
        if reverse:
            sr, sc, before_eq = n - 1 - sr, n - 1 - sc, s >= r
        sr, sc = np.broadcast_to(sr, (R, R)), np.broadcast_to(sc, (R, R))
        code = np.full((R, R), 3, np.int32)
        code[(sr >= 2) & (sc <= 1)] = 2
        code[(sr == sc + 1) & (sr % 2 == 1)] = 1
        code[(sr == sc) & before_eq] = 0
        codes.append(code)
        tris.append((code == 0).astype(np.float32))
    return jnp.asarray(np.stack(tris), BF16), jnp.asarray(np.stack(codes), I32)


def _gla_dir(q_ref, k_ref, v_ref, z_ref, w2h, w2l, b2, tri, code, st_ref, reverse, scale):
    C, n = GLA_CHUNK, GLA_STEP_CHUNKS
    R = C * n
    zh, zl = _split2(z_ref[...])
    pre = _dot(zl, w2h) + _dot(zh, w2l) + _dot(zh, w2h) + b2
    la = _log_sigmoid(pre) / GLA_TAU
    lh, lm, ll = _split3(la)
    a1 = _dot(tri, ll) + _dot(tri, lm) + _dot(tri, lh)

    last = 0 if reverse else C - 1
    order = list(range(n))[::-1] if reverse else list(range(n))
    cend = [a1[c * C + last:c * C + last + 1, :] for c in range(n)]
    c0, c1, c2, c3 = (cend[c] for c in order)
    e_before = {order[1]: c0, order[2]: c0 + c1, order[3]: c0 + c1 + c2}
    e_after = {order[0]: c1 + c2 + c3, order[1]: c2 + c3, order[2]: c3}
    etot = c0 + c1 + c2 + c3

    q_in, k_in, k_end, q_st, k_st, q_mid, k_mid = ([] for _ in range(7))
    for c in range(n):
        rows = slice(c * C, (c + 1) * C)
        a = a1[rows, :]
        qi = (q_ref[rows, :].astype(F32) * scale) * jnp.exp(a)
        k32 = k_ref[rows, :].astype(F32)
        ke = k32 * jnp.exp(cend[c] - a)
        q_in.append(qi)
        k_in.append(k32 * jnp.exp(-a))
        k_end.append(ke)
        q_st.append(qi * jnp.exp(e_before[c]) if c in e_before else qi)
        k_st.append(ke * jnp.exp(e_after[c]) if c in e_after else ke)
        q_mid.append(qi * jnp.exp(c2) if c == order[3] else qi)
        k_mid.append(ke * jnp.exp(c1) if c == order[0] else ke)
    cat = lambda xs: jnp.concatenate([x.astype(BF16) for x in xs], axis=0)
    q_in, q_st, k_st, q_mid, k_mid = cat(q_in), cat(q_st), cat(k_st), cat(q_mid), cat(k_mid)
    k_both = cat(k_in + k_end)

    s01 = _dot_nt(q_in, k_both)
    s2 = _dot_nt(q_mid, k_mid)
    p = jnp.where(code == 0, s01[:, :R], jnp.where(code == 1, s01[:, R:], jnp.where(code == 2, s2, 0.0)))
    st = st_ref[...]
    v = v_ref[...]
    o = _dot(p.astype(BF16), v) + _dot_nt(q_st, st.astype(BF16))
    st_ref[...] = st * jnp.exp(etot) + _dot_tn(v, k_st)
    return o


def _gla_kernel(*refs, nsteps, scale, has_init, n_aliased, has_final):
    qf, kf, vf, zf, qb, kb, vb, zb, w2h_ref, w2l_ref, b2_ref, gn_ref, g_ref, tri_ref, code_ref = refs[:15]
    rest = list(refs[15:])
    s0_ref = rest.pop(0) if has_init else None
    rest = rest[n_aliased:]
    o_ref = rest.pop(0)
    sfin_ref = rest.pop(0) if has_final else None
    o_acc, stf, stb = rest

    c = pl.program_id(2)
    R = GLA_CHUNK * GLA_STEP_CHUNKS

    @pl.when(c == 0)
    def _():
        o_acc[...] = jnp.zeros_like(o_acc)
        if has_init:
            stf[...] = s0_ref[0].T
            stb[...] = s0_ref[1].T
        else:
            stf[...] = jnp.zeros_like(stf)
            stb[...] = jnp.zeros_like(stb)

    of = _gla_dir(qf, kf, vf, zf, w2h_ref[0], w2l_ref[0], b2_ref[0], tri_ref[0], code_ref[0], stf, False, scale)
    ob = _gla_dir(qb, kb, vb, zb, w2h_ref[1], w2l_ref[1], b2_ref[1], tri_ref[1], code_ref[1], stb, True, scale)
    o_acc[pl.ds(pl.multiple_of(c * R, R), R), :] += of
    o_acc[pl.ds(pl.multiple_of((nsteps - 1 - c) * R, R), R), :] += ob

    @pl.when(c == nsteps - 1)
    def _():
        o = _rms(o_acc[...]) * gn_ref[...]
        o_ref[...] = (o * _silu(g_ref[...].astype(F32))).astype(o_ref.dtype)
        if has_final:
            sfin_ref[0] = stf[...].T
            sfin_ref[1] = stb[...].T


def _gla(proj, z, w2h, w2l, b2, gla_norm, tri, code, *, row_off, n_seq, L, H, HK, HV, col_q, col_k, col_v, col_g,
         s0=None, s0_layer=0, o_prev=None, ns_prev=None, ns_shape=None, ns_layer=0):
    t = proj.shape[0]
    R = GLA_CHUNK * GLA_STEP_CHUNKS
    nsteps = L // R
    rb0 = row_off // R
    has_init = s0 is not None
    has_final = ns_shape is not None
    fwd = lambda b, h, c: rb0 + b * nsteps + c
    bwd = lambda b, h, c: rb0 + b * nsteps + nsteps - 1 - c
    whole = lambda shape: pl.BlockSpec(shape, lambda b, h, c: (0,) * len(shape))

    def specs(rowmap):
        return [pl.BlockSpec((R, HK), lambda b, h, c: (rowmap(b, h, c), col_q + h)),
                pl.BlockSpec((R, HK), lambda b, h, c: (rowmap(b, h, c), col_k + h)),
                pl.BlockSpec((R, HV), lambda b, h, c: (rowmap(b, h, c), col_v + h)),
                pl.BlockSpec((R, LANES), lambda b, h, c: (rowmap(b, h, c), 0))]

    in_specs = specs(fwd) + specs(bwd) + [
        pl.BlockSpec((2, LANES, HK), lambda b, h, c: (0, 0, h)),
        pl.BlockSpec((2, LANES, HK), lambda b, h, c: (0, 0, h)),
        pl.BlockSpec((2, 1, HK), lambda b, h, c: (0, 0, h)),
        whole((1, HV)),
        pl.BlockSpec((L, HV), lambda b, h, c: (row_off // L + b, col_g + h)),
        whole((2, R, R)),
        whole((2, R, R)),
    ]
    args = [proj, proj, proj, z, proj, proj, proj, z, w2h, w2l, b2, gla_norm.reshape(1, HV), proj, tri, code]
    if has_init:
        in_specs.append(pl.BlockSpec((None, None, 2, None, HK, HV), lambda b, h, c: (b, s0_layer, 0, h, 0, 0)))
        args.append(s0)
    aliases = {}
    out_shape = [jax.ShapeDtypeStruct((t, H * HV), BF16)]
    out_specs = [pl.BlockSpec((L, HV), lambda b, h, c: (row_off // L + b, h))]
    if o_prev is not None:
        aliases[len(args)] = 0
        in_specs.append(pl.BlockSpec(memory_space=pl.ANY))
        args.append(o_prev)
    if has_final:
        out_shape.append(jax.ShapeDtypeStruct(ns_shape, F32))
        out_specs.append(pl.BlockSpec((None, None, 2, None, HK, HV), lambda b, h, c: (b, ns_layer, 0, h, 0, 0)))
        if ns_prev is not None:
            aliases[len(args)] = 1
            in_specs.append(pl.BlockSpec(memory_space=pl.ANY))
            args.append(ns_prev)
    return pl.pallas_call(
        functools.partial(_gla_kernel, nsteps=nsteps, scale=float(HK) ** -0.5,
                          has_init=has_init, n_aliased=len(aliases), has_final=has_final),
        out_shape=tuple(out_shape),
        grid=(n_seq, H, nsteps),
        in_specs=in_specs,
        out_specs=tuple(out_specs),
        scratch_shapes=[pltpu.VMEM((L, HV), F32), pltpu.VMEM((HV, HK), F32), pltpu.VMEM((HV, HK), F32)],
        input_output_aliases=aliases,
        compiler_params=_cp("arbitrary", "arbitrary", "arbitrary"),
        name="gla_init" if has_init else "gla_zero",
    )(*args)


def _branch_kernel(u_ref, og_ref, ga_ref, gb_ref, wc_ref, wg_ref, m_ref):
    a = _dot(u_ref[...], wc_ref[...])
    b = _dot(og_ref[...], wg_ref[...])
    m = _sigmoid(ga_ref[...].astype(F32)) * a + _sigmoid(gb_ref[...].astype(F32)) * b
    m_ref[...] = m.astype(m_ref.dtype)


def _branches(u_act, og, proj, w_conv_out, w_gla_out, col_ga, col_gb):
    t, c = u_act.shape
    d = w_conv_out.shape[1]
    dv = w_gla_out.shape[0]
    bm = 512
    return pl.pallas_call(
        _branch_kernel,
        out_shape=jax.ShapeDtypeStruct((t, d), BF16),
        grid=(t // bm,),
        in_specs=[
            pl.BlockSpec((bm, c), lambda i: (i, 0)),
            pl.BlockSpec((bm, dv), lambda i: (i, 0)),
            pl.BlockSpec((bm, d), lambda i: (i, col_ga)),
            pl.BlockSpec((bm, d), lambda i: (i, col_gb)),
            pl.BlockSpec((c, d), lambda i: (0, 0)),
            pl.BlockSpec((dv, d), lambda i: (0, 0)),
        ],
        out_specs=pl.BlockSpec((bm, d), lambda i: (i, 0)),
        compiler_params=_cp("arbitrary"),
        name="branches",
    )(u_act, og, proj, proj, w_conv_out, w_gla_out)


def _mix_out_kernel(m_ref, w_ref, x_ref, g1_ref, nw_ref, sh_ref, sc_ref, wrh_ref, wrl_ref, xo_ref, h_ref, lg_ref):
    x = x_ref[...] + g1_ref[...] * _dot(m_ref[...], w_ref[...])
    xo_ref[...] = x
    h = (_rms(x) * nw_ref[...]) * (1.0 + sc_ref[...]) + sh_ref[...]
    hh, hl = _split2(h)
    h_ref[...] = hh
    lg_ref[...] = _dot_nt(wrl_ref[...], hh) + _dot_nt(wrh_ref[...], hl) + _dot_nt(wrh_ref[...], hh)


def _mix_out(merged, w_mix, x, mods, norm_w, wr_hi, wr_lo, S):
    t, d = x.shape
    e = wr_hi.shape[0]
    bm = 512
    seg = lambda i: (i * bm) // S
    mod = lambda which: pl.BlockSpec((None, 1, d), lambda i: (seg(i) * N_MOD + which, 0, 0))
    return pl.pallas_call(
        _mix_out_kernel,
        out_shape=(jax.ShapeDtypeStruct((t, d), F32), jax.ShapeDtypeStruct((t, d), BF16),
                   jax.ShapeDtypeStruct((e, t), F32)),
        grid=(t // bm,),
        in_specs=[
            pl.BlockSpec((bm, d), lambda i: (i, 0)),
            pl.BlockSpec((d, d), lambda i: (0, 0)),
            pl.BlockSpec((bm, d), lambda i: (i, 0)),
            mod(2),
            pl.BlockSpec((1, d), lambda i: (0, 0)),
            mod(3), mod(4),
            pl.BlockSpec((e, d), lambda i: (0, 0)),
            pl.BlockSpec((e, d), lambda i: (0, 0)),
        ],
        out_specs=(pl.BlockSpec((bm, d), lambda i: (i, 0)), pl.BlockSpec((bm, d), lambda i: (i, 0)),
                   pl.BlockSpec((e, bm), lambda i: (0, i))),
        compiler_params=_cp("arbitrary"),
        name="mix_out",
    )(merged, w_mix, x, mods, norm_w, mods, mods, wr_hi, wr_lo)


def _route_kernel(lg_ref, slot_ref, gate_ref, cnt_ref, *, n_sets, L, cap):
    lg = lg_ref[...]
    e = lg.shape[0]
    ex = jnp.exp(lg - jnp.max(lg, axis=0, keepdims=True))
    aff = ex / jnp.sum(ex, axis=0, keepdims=True)
    bits = pltpu.bitcast(aff, I32)
    sets = [bits[:, s * L:(s + 1) * L] for s in range(n_sets)]

    def body(it, vs):
        bit = jnp.left_shift(jnp.int32(1), 30 - it)
        out = []
        for a, v in zip(sets, vs):
            cand = v | bit
            cnt = jnp.sum(jnp.where(a >= cand, 1.0, 0.0), axis=1, keepdims=True)
            out.append(jnp.where(cnt >= cap, cand, v))
        return tuple(out)

    thr = lax.fori_loop(0, 31, body, tuple(jnp.zeros((e, 1), I32) for _ in range(n_sets)))

    pb = ROUTE_BLOCK
    r = lax.broadcasted_iota(I32, (pb, pb), 0)
    c = lax.broadcasted_iota(I32, (pb, pb), 1)
    tri = jnp.where(r <= c, 1.0, 0.0).astype(BF16)
    cnt_ref[...] = jnp.full(cnt_ref.shape, n_sets * cap, I32)
    for s in range(n_sets):
        a, v = sets[s], thr[s]
        n_gt = jnp.sum(jnp.where(a > v, 1.0, 0.0), axis=1, keepdims=True)
        need = cap - n_gt
        carry_eq = jnp.zeros((e, 1), F32)
        carry_sel = jnp.zeros((e, 1), F32)
        for b0 in range(0, L, pb):
            blk = (s * L + b0) // pb
            cnt_ref[:, blk:blk + 1] = carry_sel.astype(I32) + s * cap
            ab = a[:, b0:b0 + pb]
            eq = jnp.where(ab == v, 1.0, 0.0)
            rank_eq = _dot(eq.astype(BF16), tri) - eq + carry_eq
            sel = jnp.where(ab > v, 1.0, jnp.where(rank_eq < need, eq, 0.0))
            rank_sel = _dot(sel.astype(BF16), tri) - sel + carry_sel
            carry_eq = carry_eq + jnp.sum(eq, axis=1, keepdims=True)
            carry_sel = carry_sel + jnp.sum(sel, axis=1, keepdims=True)
            cols = slice(s * L + b0, s * L + b0 + pb)
            slot_ref[:, cols] = jnp.where(sel > 0.0, rank_sel.astype(I32) + s * cap, -1)
            gate_ref[:, cols] = jnp.where(sel > 0.0, aff[:, cols], 0.0)


def _route(logits_t, *, col0, n_groups, n_sets, L, cap):
    e = logits_t.shape[0]
    n = n_sets * L
    assert n // ROUTE_BLOCK < LANES
    return pl.pallas_call(
        functools.partial(_route_kernel, n_sets=n_sets, L=L, cap=cap),
        out_shape=(jax.ShapeDtypeStruct((e, n_groups * n), I32), jax.ShapeDtypeStruct((e, n_groups * n), F32),
                   jax.ShapeDtypeStruct((e, n_groups * LANES), I32)),
        grid=(n_groups,),
        in_specs=[pl.BlockSpec((e, n), lambda g: (0, col0 + g))],
        out_specs=(pl.BlockSpec((e, n), lambda g: (0, g)), pl.BlockSpec((e, n), lambda g: (0, g)),
                   pl.BlockSpec((e, LANES), lambda g: (0, g))),
        compiler_params=_cp("arbitrary"),
        name="route",
    )(logits_t)


GATHER_TOKENS = 512
GATHER_WINDOW = 128
GATHER_GROUP = 2
SCATTER_TOKENS = 512
SCATTER_WINDOW = 128
BF16_ROWS = 16


def _gather_kernel(cnt_ref, h_ref, slot_ref, gate_ref, g_ref, gs_ref, acc, gs_acc, *, nslot, n_exp, n_cnt):
    s_tok = h_ref.shape[0]
    tk = min(GATHER_TOKENS, s_tok)
    w = min(GATHER_WINDOW, nslot)
    grp = acc.shape[0]
    seg, g0 = pl.program_id(0), pl.program_id(1) * grp
    acc[...] = jnp.zeros_like(acc)
    gs_acc[...] = jnp.zeros_like(gs_acc)

    def onehot(el, c, start, lower=None):
        ids = lax.broadcasted_iota(I32, (w, tk), 0) + start
        hit = slot_ref[el, :, c * tk:(c + 1) * tk] == ids
        oh = jnp.where(hit, 1.0, 0.0)
        gate = jnp.where(hit, gate_ref[el, :, c * tk:(c + 1) * tk], 0.0)
        if lower is not None:
            oh = jnp.where(ids >= lower, oh, 0.0)
            gate = jnp.where(ids >= lower, gate, 0.0)
        return oh.astype(BF16), jnp.sum(gate, axis=1, keepdims=True)

    for c in range(s_tok // tk):
        starts, his, pieces = [], [], []
        for el in range(grp):
            base = (seg * n_exp + g0 + el) * n_cnt
            lo = cnt_ref[base + c * (tk // ROUTE_BLOCK)]
            his.append(cnt_ref[base + (c + 1) * (tk // ROUTE_BLOCK)])
            starts.append(pl.multiple_of(jnp.minimum((lo // SUBLANES) * SUBLANES, nslot - w), SUBLANES))
            oh, gsum = onehot(el, c, starts[el])
            pieces.append(oh)
            gs_acc[el, pl.ds(starts[el], w), :] += gsum
        part = _dot(jnp.concatenate(pieces, axis=0), h_ref[c * tk:(c + 1) * tk, :])
        for el in range(grp):
            acc[el, pl.ds(starts[el], w), :] += part[el * w:(el + 1) * w, :]

        for el in range(grp):
            n_extra = (jnp.maximum(his[el] - starts[el], 1) - 1) // w

            def body(p, carry, el=el, c=c, start=starts[el]):
                lower = start + (p + 1) * w
                st = pl.multiple_of(jnp.minimum(lower, nslot - w), SUBLANES)
                oh, gsum = onehot(el, c, st, lower)
                acc[el, pl.ds(st, w), :] += _dot(oh, h_ref[c * tk:(c + 1) * tk, :])
                gs_acc[el, pl.ds(st, w), :] += gsum
                return carry

            lax.fori_loop(0, n_extra, body, 0)
    g_ref[...] = acc[...].astype(g_ref.dtype)
    gs_ref[...] = gs_acc[...]


def _gather(cnt, h2, slot3, gate3, S, nslot, n_cnt):
    t, d = h2.shape
    e = slot3.shape[0]
    nseg = t // S
    grp = GATHER_GROUP
    assert e % grp == 0 and nslot % SUBLANES == 0
    return pl.pallas_call(
        functools.partial(_gather_kernel, nslot=nslot, n_exp=e, n_cnt=n_cnt),
        out_shape=(jax.ShapeDtypeStruct((e, nseg * nslot, d), BF16), jax.ShapeDtypeStruct((e, nseg * nslot, 1), F32)),
        grid_spec=pltpu.PrefetchScalarGridSpec(
            num_scalar_prefetch=1,
            grid=(nseg, e // grp),
            in_specs=[
                pl.BlockSpec((S, d), lambda s, x, cnt: (s, 0), pipeline_mode=pl.Buffered(1)),
                pl.BlockSpec((grp, 1, S), lambda s, x, cnt: (x, 0, s)),
                pl.BlockSpec((grp, 1, S), lambda s, x, cnt: (x, 0, s)),
            ],
            out_specs=(pl.BlockSpec((grp, nslot, d), lambda s, x, cnt: (x, s, 0)),
                       pl.BlockSpec((grp, nslot, 1), lambda s, x, cnt: (x, s, 0))),
            scratch_shapes=[pltpu.VMEM((grp, nslot, d), F32), pltpu.VMEM((grp, nslot, 1), F32)],
        ),
        compiler_params=_cp("arbitrary", "arbitrary"),
        name="moe_gather",
    )(cnt, h2, slot3, gate3)


def _ffn_up_kernel(g_ref, wg_ref, wu_ref, a_ref):
    g = g_ref[...]
    a = _dot(g, wg_ref[...].astype(BF16))
    b = _dot(g, wu_ref[...].astype(BF16))
    a_ref[...] = (_silu(a) * b).astype(a_ref.dtype)


def _ffn_up(gathered, w_gate, w_up, layer):
    e, m, d = gathered.shape
    f = w_gate.shape[3]
    bf = 512
    return pl.pallas_call(
        _ffn_up_kernel,
        out_shape=jax.ShapeDtypeStruct((e, m, f), BF16),
        grid=(e, f // bf),
        in_specs=[
            pl.BlockSpec((None, m, d), lambda x, j: (x, 0, 0)),
            pl.BlockSpec((None, None, d, bf), lambda x, j: (layer, x, 0, j)),
            pl.BlockSpec((None, None, d, bf), lambda x, j: (layer, x, 0, j)),
        ],
        out_specs=pl.BlockSpec((None, m, bf), lambda x, j: (x, 0, j)),
        compiler_params=_cp("arbitrary", "arbitrary"),
        name="ffn_up",
    )(gathered, w_gate, w_up)


def _ffn_down_kernel(a_ref, wd_ref, gs_ref, y_ref):
    y = _dot(a_ref[...], wd_ref[...].astype(BF16))
    y_ref[...] = (y * gs_ref[...]).astype(y_ref.dtype)


def _ffn_down(act, w_down, gslot, layer):
    e, m, f = act.shape
    d = w_down.shape[3]
    bn = 512
    return pl.pallas_call(
        _ffn_down_kernel,
        out_shape=jax.ShapeDtypeStruct((e, m, d), BF16),
        grid=(e, d // bn),
        in_specs=[
            pl.BlockSpec((None, m, f), lambda x, j: (x, 0, 0)),
            pl.BlockSpec((None, None, f, bn), lambda x, j: (layer, x, 0, j)),
            pl.BlockSpec((None, m, 1), lambda x, j: (x, 0, 0)),
        ],
        out_specs=pl.BlockSpec((None, m, bn), lambda x, j: (x, 0, j)),
        compiler_params=_cp("arbitrary", "arbitrary"),
        name="ffn_down",
    )(act, w_down, gslot)


def _scatter_kernel(cnt_ref, slot_ref, x_ref, g2_ref, y_hbm, o_ref, ycat, ybuf, sems, sem_extra,
                    *, nslot, n_exp, n_cnt, S):
    i = pl.program_id(0)
    tb = o_ref.shape[0]
    w = min(SCATTER_WINDOW, nslot)
    half = n_exp // 2
    cur = lax.rem(i, 2)

    def bounds(step, e):
        seg = (step * tb) // S
        base = (seg * n_exp + e) * n_cnt + (step * tb - seg * S) // ROUTE_BLOCK
        lo = cnt_ref[base]
        start = pl.multiple_of(jnp.minimum((lo // BF16_ROWS) * BF16_ROWS, nslot - w), BF16_ROWS)
        return seg * nslot, start, cnt_ref[base + tb // ROUTE_BLOCK]

    def window(e, row0, start, dst, sem):
        return pltpu.make_async_copy(y_hbm.at[e, pl.ds(row0 + start, w), :], dst, sem)

    def main_window(step, buf, e):
        row0, start, _ = bounds(step, e)
        return window(e, row0, start, ycat.at[buf, pl.ds(e * w, w), :], sems.at[buf, e])

    @pl.when(i == 0)
    def _():
        for e in range(n_exp):
            main_window(0, 0, e).start()

    @pl.when(i + 1 < pl.num_programs(0))
    def _():
        for e in range(n_exp):
            main_window(i + 1, 1 - cur, e).start()

    row0, _, _ = bounds(i, 0)
    starts = [bounds(i, e)[1] for e in range(n_exp)]
    his = [bounds(i, e)[2] for e in range(n_exp)]

    def onehot(e, start, lower=None):
        ids = lax.broadcasted_iota(I32, (w, tb), 0) + start
        oh = jnp.where(slot_ref[e] == ids, 1.0, 0.0)
        if lower is not None:
            oh = jnp.where(ids >= lower, oh, 0.0)
        return oh.astype(BF16)

    for g in range(2):
        oh = jnp.concatenate([onehot(e, starts[e]) for e in range(g * half, (g + 1) * half)], axis=0)
        for e in range(g * half, (g + 1) * half):
            main_window(i, cur, e).wait()
        part = _dot_tn(oh, ycat[cur, g * half * w:(g + 1) * half * w, :])
        if g == 0:
            o_ref[...] = part
        else:
            o_ref[...] += part

    for e in range(n_exp):
        n_extra = (jnp.maximum(his[e] - starts[e], 1) - 1) // w

        def body(p, carry, e=e):
            lower = starts[e] + (p + 1) * w
            st = pl.multiple_of(jnp.minimum(lower, nslot - w), BF16_ROWS)
            cp = window(e, row0, st, ybuf, sem_extra.at[0])
            cp.start()
            cp.wait()
            o_ref[...] += _dot_tn(onehot(e, st, lower), ybuf[...])
            return carry

        lax.fori_loop(0, n_extra, body, 0)

    o_ref[...] = x_ref[...] + g2_ref[...] * o_ref[...]


def _scatter(cnt, slot3, yg, x, mods, S, nslot, n_cnt):
    t, d = x.shape
    e = slot3.shape[0]
    tb = min(SCATTER_TOKENS, S)
    w = min(SCATTER_WINDOW, nslot)
    assert e % 2 == 0 and nslot % BF16_ROWS == 0
    seg = lambda i: (i * tb) // S
    return pl.pallas_call(
        functools.partial(_scatter_kernel, nslot=nslot, n_exp=e, n_cnt=n_cnt, S=S),
        out_shape=jax.ShapeDtypeStruct((t, d), F32),
        grid_spec=pltpu.PrefetchScalarGridSpec(
            num_scalar_prefetch=1,
            grid=(t // tb,),
            in_specs=[
                pl.BlockSpec((e, 1, tb), lambda i, cnt: (0, 0, i)),
                pl.BlockSpec((tb, d), lambda i, cnt: (i, 0)),
                pl.BlockSpec((None, 1, d), lambda i, cnt: (seg(i) * N_MOD + 5, 0, 0)),
                pl.BlockSpec(memory_space=pl.ANY),
            ],
            out_specs=pl.BlockSpec((tb, d), lambda i, cnt: (i, 0)),
            scratch_shapes=[pltpu.VMEM((2, e * w, d), BF16), pltpu.VMEM((w, d), BF16),
                            pltpu.SemaphoreType.DMA((2, e)), pltpu.SemaphoreType.DMA((1,))],
        ),
        compiler_params=_cp("arbitrary"),
        name="moe_scatter",
    )(cnt, slot3, x, mods, yg)


def _final_norm_kernel(x_ref, w_ref, o_ref):
    o_ref[...] = _rms(x_ref[...]) * w_ref[...]


def _final_norm(x, w, row_off, rows):
    d = x.shape[1]
    bm = 512
    off = row_off // bm
    return pl.pallas_call(
        _final_norm_kernel,
        out_shape=jax.ShapeDtypeStruct((rows, d), F32),
        grid=(rows // bm,),
        in_specs=[pl.BlockSpec((bm, d), lambda i: (off + i, 0)), pl.BlockSpec((1, d), lambda i: (0, 0))],
        out_specs=pl.BlockSpec((bm, d), lambda i: (i, 0)),
        compiler_params=_cp("arbitrary"),
        name="final_norm",
    )(x, w)


def kernel(x_prompt, x_sample, state_gla, c, c_ctx, w_in, conv_w, conv_b, conv_norm, w_conv_out, gla_gate_w2, gla_gate_b, gla_norm, w_gla_out, w_mix_out, w_ada, b_ada, norm_mix, norm_ffn, w_router, w_exp_gate, w_exp_up, w_exp_down, norm_final):
    B, SEQ, D = x_prompt.shape
    DB, DSEQ, _ = x_sample.shape
    DEPTH = w_in.shape[0]
    H, HK, HV = state_gla.shape[3:]
    DC = conv_w.shape[2]
    DK, DV = H * HK, H * HV
    RANK = gla_gate_w2.shape[2]
    E = w_router.shape[2]
    S = DSEQ
    R = GLA_CHUNK * GLA_STEP_CHUNKS
    assert B * SEQ == S, "prompt tokens must fill exactly one segment"
    assert 2 * RANK <= LANES and conv_w.shape[1] // 2 <= CONV_HALO
    assert SEQ % R == 0 and DSEQ % R == 0 and SEQ % ROUTE_BLOCK == 0 and S % GATHER_TOKENS == 0
    nseg = 1 + DB
    assert nseg <= 8
    T = nseg * S
    cap_p = (EC_CAPACITY_FACTOR * SEQ) // E
    cap_s = (EC_CAPACITY_FACTOR * DSEQ) // E
    nslot = B * cap_p
    assert nslot == cap_s
    n_cnt = S // ROUTE_BLOCK + 1

    x = _embed(x_prompt.reshape(B * SEQ, D), x_sample.reshape(DB * DSEQ, D), _pos_table(DSEQ, D), S)

    cond8 = jnp.zeros((8, D), F32).at[0].set(c_ctx).at[1:1 + DB].set(c)
    mods_all = _adaln(cond8, w_ada, b_ada).reshape(DEPTH, 8 * N_MOD, 1, D)

    z0 = 2 * DC + 2 * DK + 2 * DV
    col_q, col_k = (2 * DC) // HK, (2 * DC + DK) // HK
    col_v = (2 * DC + 2 * DK) // HV
    col_g = (2 * DC + 2 * DK + DV) // HV
    col_ga, col_gb = z0 // D, z0 // D + 1
    tri, code = _gla_masks()
    w_in_t = jnp.swapaxes(w_in, 1, 2)

    new_state = None
    ns_shape = (B, DEPTH, 2, H, HK, HV)
    for l in range(DEPTH):
        mods = mods_all[l]
        proj, z = _in_proj(x, norm_mix[l].reshape(1, D), mods, w_in_t, l, z0, 2 * RANK, S)

        u_act = _conv_branch(proj, conv_w[l], conv_b[l], conv_norm[l], S, SEQ, DSEQ)

        w2pad = jnp.zeros((2, LANES, DK), F32)
        w2pad = w2pad.at[0, :RANK].set(gla_gate_w2[l, 0]).at[1, RANK:2 * RANK].set(gla_gate_w2[l, 1])
        w2h = w2pad.astype(BF16)
        w2l = (w2pad - w2h.astype(F32)).astype(BF16)
        b2 = gla_gate_b[l].reshape(2, 1, DK)
        common = dict(H=H, HK=HK, HV=HV, col_q=col_q, col_k=col_k, col_v=col_v, col_g=col_g)
        og, new_state = _gla(proj, z, w2h, w2l, b2, gla_norm[l], tri, code, row_off=0, n_seq=B, L=SEQ,
                             ns_prev=new_state, ns_shape=ns_shape, ns_layer=l, **common)
        (og,) = _gla(proj, z, w2h, w2l, b2, gla_norm[l], tri, code, row_off=S, n_seq=DB, L=DSEQ,
                     s0=state_gla, s0_layer=l, o_prev=og, **common)

        merged = _branches(u_act, og, proj, w_conv_out[l].astype(BF16), w_gla_out[l].astype(BF16), col_ga, col_gb)
        wr_t = w_router[l].T
        wr_hi = wr_t.astype(BF16)
        wr_lo = (wr_t - wr_hi.astype(F32)).astype(BF16)
        x, h2, logits_t = _mix_out(merged, w_mix_out[l].astype(BF16), x, mods, norm_ffn[l].reshape(1, D),
                                   wr_hi, wr_lo, S)

        slot_p, gate_p, cnt_p = _route(logits_t, col0=0, n_groups=1, n_sets=B, L=SEQ, cap=cap_p)
        slot_s, gate_s, cnt_s = _route(logits_t, col0=1, n_groups=DB, n_sets=1, L=DSEQ, cap=cap_s)
        slot3 = jnp.concatenate([slot_p, slot_s], axis=1).reshape(E, 1, T)
        gate3 = jnp.concatenate([gate_p, gate_s], axis=1).reshape(E, 1, T)
        cnt = jnp.concatenate([cnt_p, cnt_s], axis=1).reshape(E, nseg, LANES)[:, :, :n_cnt]
        cnt = cnt.transpose(1, 0, 2).reshape(nseg * E * n_cnt)

        gathered, gslot = _gather(cnt, h2, slot3, gate3, S, nslot, n_cnt)
        act = _ffn_up(gathered, w_exp_gate, w_exp_up, l)
        yg = _ffn_down(act, w_exp_down, gslot, l)
        x = _scatter(cnt, slot3, yg, x, mods, S, nslot, n_cnt)

    nf = norm_final.reshape(1, D)
    y_prompt = _final_norm(x, nf, 0, B * SEQ).reshape(B, SEQ, D)
    y_sample = _final_norm(x, nf, S, DB * DSEQ).reshape(DB, DSEQ, D)
    return (y_prompt, y_sample, new_state)
```
